```python
import jax, jax.numpy as jnp
from jax import lax
import numpy as np

D_MODEL = 1024
BATCH = 8
SEQ = 2048
DEPTH = 2

N_MIXERS = 2
GMLP_CHUNK = 128
GMLP_WIDTH = 2 * D_MODEL
GMLP_GROUPS = 8
GMLP_GROUP_DIM = GMLP_WIDTH // GMLP_GROUPS
N_HEADS = 8
HEAD_DIM = D_MODEL // N_HEADS
MOBA_BLOCK = 256
MOBA_TOPK = 3
MOBA_QCHUNK = 16
ROPE_THETA = 10000.0
D_FF = 2816
CONV_WIDTH = 3

EPS = 1e-6
NEG_INF = -1e30
N_A_LAYERS = (DEPTH + 1) // 2
N_B_LAYERS = DEPTH // 2

kernel_name = "hybrid_gmlp_moba_convffn"


def rms_norm(x, g):
    xf = x.astype(jnp.float32)
    y = xf * lax.rsqrt(jnp.mean(xf * xf, axis=-1, keepdims=True) + EPS)
    return (y * g.astype(jnp.float32)).astype(x.dtype)


def rope(x, positions):
    half = HEAD_DIM // 2
    inv = ROPE_THETA ** (-jnp.arange(half, dtype=jnp.float32) / half)
    ang = positions.astype(jnp.float32)[:, None] * inv[None, :]
    cos = jnp.cos(ang)[None, :, None, :]
    sin = jnp.sin(ang)[None, :, None, :]
    xf = x.astype(jnp.float32)
    x1, x2 = xf[..., :half], xf[..., half:]
    return jnp.concatenate([x1 * cos - x2 * sin, x2 * cos + x1 * sin], axis=-1).astype(x.dtype)


def gmlp_mixer(h, w_in, v_gain, w_s, b_s, w_out):
    B, S, _ = h.shape
    z = jax.nn.gelu(h @ w_in)
    u, v = jnp.split(z, 2, axis=-1)
    v = rms_norm(v, v_gain)
    v = v.reshape(B, S // GMLP_CHUNK, GMLP_CHUNK, GMLP_GROUPS, GMLP_GROUP_DIM)
    causal = jnp.tril(jnp.ones((GMLP_CHUNK, GMLP_CHUNK), dtype=bool))
    ws = jnp.where(causal[None], w_s, jnp.zeros_like(w_s)).astype(v.dtype)
    s = jnp.einsum('gts,bcsgd->bctgd', ws, v) + b_s.T[None, None, :, :, None].astype(v.dtype)
    return (u * s.reshape(B, S, GMLP_WIDTH)) @ w_out


def moba_mixer(h, w_qkv, w_o):
    B, S, _ = h.shape
    qkv = (h @ w_qkv).reshape(B, S, 3, N_HEADS, HEAD_DIM)
    pos = jnp.arange(S)
    q = rope(qkv[:, :, 0], pos)
    k = rope(qkv[:, :, 1], pos)
    v = qkv[:, :, 2]
    pad = (-S) % MOBA_BLOCK
    Sp = S + pad
    NB = Sp // MOBA_BLOCK
    q, k, v = [jnp.pad(t, ((0, 0), (0, pad), (0, 0), (0, 0))).transpose(0, 2, 1, 3) for t in (q, k, v)]
    kb = k.reshape(B, N_HEADS, NB, MOBA_BLOCK, HEAD_DIM)
    vb = v.reshape(B, N_HEADS, NB, MOBA_BLOCK, HEAD_DIM)
    q_blk = jnp.arange(Sp) // MOBA_BLOCK
    n_sel = min(MOBA_TOPK, NB - 1)
    NQ = Sp // MOBA_QCHUNK
    scale = HEAD_DIM ** -0.5
    q_c = q.reshape(B, N_HEADS, NQ, MOBA_QCHUNK, HEAD_DIM).transpose(2, 0, 1, 3, 4)
    chunk_ids = jnp.arange(NQ)

    if n_sel > 0:
        k_mean = jnp.mean(kb.astype(jnp.float32), axis=3)
        gate = jnp.einsum('bhsd,bhnd->bhsn', q.astype(jnp.float32), k_mean)
        past = jnp.arange(NB)[None, :] < q_blk[:, None]
        gate = jnp.where(past[None, None], gate, NEG_INF)
        _, sel = lax.top_k(gate, n_sel)
        sel_valid = jnp.arange(n_sel)[None, :] < q_blk[:, None]
        sel_c = sel.reshape(B, N_HEADS, NQ, MOBA_QCHUNK, n_sel).transpose(2, 0, 1, 3, 4)
        valid_c = sel_valid.reshape(NQ, MOBA_QCHUNK, n_sel)
    else:
        sel_c = jnp.zeros((NQ, B, N_HEADS, MOBA_QCHUNK, 0), dtype=jnp.int32)
        valid_c = jnp.zeros((NQ, MOBA_QCHUNK, 0), dtype=bool)

    bi = jnp.arange(B)[:, None, None, None]
    hi = jnp.arange(N_HEADS)[None, :, None, None]

    def attend(args):
        qc, selc, validc, c = args
        blk = (c * MOBA_QCHUNK) // MOBA_BLOCK
        qpos = c * MOBA_QCHUNK + jnp.arange(MOBA_QCHUNK)
        kpos = blk * MOBA_BLOCK + jnp.arange(MOBA_BLOCK)
        k_own = lax.dynamic_index_in_dim(kb, blk, axis=2, keepdims=False)
        v_own = lax.dynamic_index_in_dim(vb, blk, axis=2, keepdims=False)
        s_own = jnp.einsum('bhqd,bhkd->bhqk', qc, k_own).astype(jnp.float32) * scale
        s_own = jnp.where((kpos[None, :] <= qpos[:, None])[None, None], s_own, NEG_INF)
        if n_sel > 0:
            k_sel = kb[bi, hi, selc]
            v_sel = vb[bi, hi, selc]
            s_sel = jnp.einsum('bhqd,bhqrkd->bhqrk', qc, k_sel).astype(jnp.float32) * scale
            s_sel = jnp.where(validc[None, None, :, :, None], s_sel, NEG_INF)
            s_sel = s_sel.reshape(B, N_HEADS, MOBA_QCHUNK, n_sel * MOBA_BLOCK)
            p = jax.nn.softmax(jnp.concatenate([s_sel, s_own], axis=-1), axis=-1).astype(qc.dtype)
            p_sel = p[..., :n_sel * MOBA_BLOCK].reshape(B, N_HEADS, MOBA_QCHUNK, n_sel, MOBA_BLOCK)
            p_own = p[..., n_sel * MOBA_BLOCK:]
            return (jnp.einsum('bhqrk,bhqrkd->bhqd', p_sel, v_sel)
                    + jnp.einsum('bhqk,bhkd->bhqd', p_own, v_own))
        p_own = jax.nn.softmax(s_own, axis=-1).astype(qc.dtype)
        return jnp.einsum('bhqk,bhkd->bhqd', p_own, v_own)

    out = lax.map(attend, (q_c, sel_c, valid_c, chunk_ids))
    out = out.transpose(1, 0, 3, 2, 4).reshape(B, Sp, D_MODEL)[:, :S]
    return out @ w_o


def conv_ffn(h, w_up, conv_w, conv_b, w_down):
    a = h @ w_up
    a = lax.conv_general_dilated(
        a, conv_w[:, None, :].astype(a.dtype), window_strides=(1,),
        padding=[(CONV_WIDTH - 1, 0)], dimension_numbers=('NWC', 'WIO', 'NWC'),
        feature_group_count=2 * D_FF) + conv_b
    g, up = jnp.split(a, 2, axis=-1)
    return (jax.nn.gelu(g) * up) @ w_down


def setup_inputs(seed: int = 0) -> dict:
    key = jax.random.key(seed)
    ks = jax.random.split(key, 16)
    f32 = jnp.float32
    nrm = lambda k, shape, s: jax.random.normal(k, shape, f32) * s
    return {
        "x": nrm(ks[0], (BATCH, SEQ, D_MODEL), 1.0),
        "mix_norm": 1.0 + nrm(ks[1], (DEPTH, D_MODEL), 0.05),
        "a_w_in": nrm(ks[2], (N_A_LAYERS, D_MODEL, 2 * GMLP_WIDTH), D_MODEL ** -0.5),
        "a_v_gain": 1.0 + nrm(ks[3], (N_A_LAYERS, GMLP_WIDTH), 0.05),
        "a_w_s": nrm(ks[4], (N_A_LAYERS, GMLP_GROUPS, GMLP_CHUNK, GMLP_CHUNK), GMLP_CHUNK ** -0.5),
        "a_b_s": 1.0 + nrm(ks[5], (N_A_LAYERS, GMLP_GROUPS, GMLP_CHUNK), 0.1),
        "a_w_out": nrm(ks[6], (N_A_LAYERS, GMLP_WIDTH, D_MODEL), GMLP_WIDTH ** -0.5),
        "b_w_qkv": nrm(ks[7], (N_B_LAYERS, D_MODEL, 3 * D_MODEL), D_MODEL ** -0.5),
        "b_w_o": nrm(ks[8], (N_B_LAYERS, D_MODEL, D_MODEL), D_MODEL ** -0.5),
        "ffn_norm": 1.0 + nrm(ks[9], (DEPTH, D_MODEL), 0.05),
        "ffn_w_up": nrm(ks[10], (DEPTH, D_MODEL, 2 * D_FF), D_MODEL ** -0.5),
        "ffn_conv_w": nrm(ks[11], (DEPTH, CONV_WIDTH, 2 * D_FF), CONV_WIDTH ** -0.5),
        "ffn_conv_b": nrm(ks[12], (DEPTH, 2 * D_FF), 0.02),
        "ffn_w_down": nrm(ks[13], (DEPTH, D_FF, D_MODEL), D_FF ** -0.5),
        "final_norm": 1.0 + nrm(ks[14], (D_MODEL,), 0.05),
    }


def reference(x, mix_norm, a_w_in, a_v_gain, a_w_s, a_b_s, a_w_out, b_w_qkv, b_w_o,
              ffn_norm, ffn_w_up, ffn_conv_w, ffn_conv_b, ffn_w_down, final_norm):
    h = x
    for i in range(DEPTH):
        y = rms_norm(h, mix_norm[i])
        j = i // N_MIXERS
        if i % N_MIXERS == 0:
            h = h + gmlp_mixer(y, a_w_in[j], a_v_gain[j], a_w_s[j], a_b_s[j], a_w_out[j])
        else:
            h = h + moba_mixer(y, b_w_qkv[j], b_w_o[j])
        h = h + conv_ffn(rms_norm(h, ffn_norm[i]), ffn_w_up[i], ffn_conv_w[i], ffn_conv_b[i], ffn_w_down[i])
    return rms_norm(h, final_norm)
```

```python
import functools
import math

import jax
import jax.numpy as jnp
from jax import lax
from jax.experimental import pallas as pl
from jax.experimental.pallas import tpu as pltpu

F32 = jnp.float32
BF16 = jnp.bfloat16

EPS = 1e-6
NEG_INF = -1e30
GMLP_CHUNK = 128
GMLP_GROUPS = 8
N_HEADS = 8
HEAD_DIM = 128
MOBA_BLOCK = 256
MOBA_TOPK = 3
ROPE_THETA = 10000.0
CONV_WIDTH = 3
SUBLANES = 8
VMEM_LIMIT = 56 * 1024 * 1024


def _rms(x, g):
    return x * lax.rsqrt(jnp.mean(x * x, axis=-1, keepdims=True) + EPS) * g


def _gelu(x):
    c = math.sqrt(2.0 / math.pi)
    return x * (0.5 * (1.0 + jnp.tanh(c * (x + 0.044715 * (x * x * x)))))


def _dot(a, b):
    return jnp.dot(a, b, preferred_element_type=F32)


def _dot_nt(a, b):
    return lax.dot_general(a, b, (((1,), (1,)), ((), ())), preferred_element_type=F32)


def _const_spec(shape):
    return pl.BlockSpec(shape, lambda *_: (0,) * len(shape), pipeline_mode=pl.Buffered(1))


def _gmlp_kernel(h_ref, g_ref, win_ref, vg_ref, ws_ref, bst_ref, wout_ref, o_ref, v_scr):
    tm = h_ref.shape[0]
    width = vg_ref.shape[1]
    gdim = width // GMLP_GROUPS
    h = h_ref[...]
    y = _rms(h, g_ref[...]).astype(BF16)
    v = _gelu(_dot(y, win_ref[:, width:]))
    v = _rms(v, vg_ref[...])
    v_scr[...] = v.astype(BF16)
    t_idx = lax.broadcasted_iota(jnp.int32, (GMLP_CHUNK, GMLP_CHUNK), 0)
    s_idx = lax.broadcasted_iota(jnp.int32, (GMLP_CHUNK, GMLP_CHUNK), 1)
    causal = s_idx <= t_idx
    acc = h
    for g in range(GMLP_GROUPS):
        cols = slice(g * gdim, (g + 1) * gdim)
        u = _gelu(_dot(y, win_ref[:, cols]))
        ws = jnp.where(causal, ws_ref[g], 0.0).astype(BF16)
        bias = bst_ref[:, g:g + 1]
        s = jnp.concatenate(
            [_dot(ws, v_scr[c * GMLP_CHUNK:(c + 1) * GMLP_CHUNK, cols]) + bias
             for c in range(tm // GMLP_CHUNK)], axis=0)
        acc = acc + _dot((u * s).astype(BF16), wout_ref[cols, :])
    o_ref[...] = acc


def _gmlp_layer(h, g, w_in, v_gain, w_s, b_s, w_out, tm=256):
    n, d = h.shape
    width = v_gain.shape[-1]
    return pl.pallas_call(
        _gmlp_kernel,
        grid=(n // tm,),
        in_specs=[
            pl.BlockSpec((tm, d), lambda i: (i, 0)),
            _const_spec((1, d)),
            _const_spec((d, 2 * width)),
            _const_spec((1, width)),
            _const_spec(w_s.shape),
            _const_spec((GMLP_CHUNK, GMLP_GROUPS)),
            _const_spec((width, d)),
        ],
        out_specs=pl.BlockSpec((tm, d), lambda i: (i, 0)),
        out_shape=jax.ShapeDtypeStruct((n, d), F32),
        scratch_shapes=[pltpu.VMEM((tm, width), BF16)],
        compiler_params=pltpu.CompilerParams(
            dimension_semantics=("parallel",), vmem_limit_bytes=VMEM_LIMIT),
        name="gmlp_mixer",
    )(h, g.reshape(1, d), w_in.astype(BF16), v_gain.reshape(1, width), w_s, b_s.T,
      w_out.astype(BF16))


def _ffn_kernel(h_ref, g_ref, wup_ref, cw_ref, cb_ref, wdown_ref, fg_ref, o_ref, carry_ref,
                *, tiles_per_seq, fc, final_norm):
    tm = h_ref.shape[0]
    d_ff = wdown_ref.shape[0]

    @pl.when(pl.program_id(0) % tiles_per_seq == 0)
    def _():
        carry_ref[...] = jnp.zeros_like(carry_ref)

    h = h_ref[...]
    y = _rms(h, g_ref[...]).astype(BF16)
    row = lax.broadcasted_iota(jnp.int32, (tm, fc), 0)
    acc = h
    for j in range(d_ff // fc):
        conv = []
        for half in range(2):
            cols = slice(half * d_ff + j * fc, half * d_ff + (j + 1) * fc)
            a = _dot(y, wup_ref[:, cols])
            prev = carry_ref[:, cols]
            carry_ref[:, cols] = a[tm - SUBLANES:, :]
            a1 = jnp.where(row == 0, prev[SUBLANES - 1:SUBLANES, :], pltpu.roll(a, 1, 0))
            a2 = jnp.where(row == 0, prev[SUBLANES - 2:SUBLANES - 1, :],
                           jnp.where(row == 1, prev[SUBLANES - 1:SUBLANES, :], pltpu.roll(a, 2, 0)))
            conv.append(cw_ref[0:1, cols] * a2 + cw_ref[1:2, cols] * a1 + cw_ref[2:3, cols] * a
                        + cb_ref[:, cols])
        act = (_gelu(conv[0]) * conv[1]).astype(BF16)
        acc = acc + _dot(act, wdown_ref[j * fc:(j + 1) * fc, :])
    if final_norm:
        acc = _rms(acc, fg_ref[...])
    o_ref[...] = acc


def _ffn_layer(h, g, w_up, conv_w, conv_b, w_down, final_g, seq, final_norm, tm=512, fc=256):
    n, d = h.shape
    d_ff = w_down.shape[0]
    kern = functools.partial(_ffn_kernel, tiles_per_seq=seq // tm, fc=fc, final_norm=final_norm)
    return pl.pallas_call(
        kern,
        grid=(n // tm,),
        in_specs=[
            pl.BlockSpec((tm, d), lambda i: (i, 0)),
            _const_spec((1, d)),
            _const_spec((d, 2 * d_ff)),
            _const_spec((CONV_WIDTH, 2 * d_ff)),
            _const_spec((1, 2 * d_ff)),
            _const_spec((d_ff, d)),
            _const_spec((1, d)),
        ],
        out_specs=pl.BlockSpec((tm, d), lambda i: (i, 0)),
        out_shape=jax.ShapeDtypeStruct((n, d), F32),
        scratch_shapes=[pltpu.VMEM((SUBLANES, 2 * d_ff), F32)],
        compiler_params=pltpu.CompilerParams(
            dimension_semantics=("arbitrary",), vmem_limit_bytes=VMEM_LIMIT),
        name="conv_ffn",
    )(h, g.reshape(1, d), w_up.astype(BF16), conv_w, conv_b.reshape(1, 2 * d_ff),
      w_down.astype(BF16), final_g.reshape(1, d))


def _qkv_kernel(h_ref, g_ref, w_ref, cos_ref, sin_ref, qh_ref, ql_ref, k_ref, v_ref, km_ref):
    d = h_ref.shape[1]
    y = _rms(h_ref[...], g_ref[...]).astype(BF16)
    cos = cos_ref[...]
    sin = sin_ref[...]

    def rope(x):
        heads = []
        for hh in range(N_HEADS):
            xh = x[:, hh * HEAD_DIM:(hh + 1) * HEAD_DIM]
            heads.append(xh * cos + pltpu.roll(xh, HEAD_DIM // 2, 1) * sin)
        return jnp.concatenate(heads, axis=1)

    q = rope(_dot(y, w_ref[:, :d])) * (HEAD_DIM ** -0.5)
    qh = q.astype(BF16)
    qh_ref[...] = qh
    ql_ref[...] = (q - qh.astype(F32)).astype(BF16)
    k = rope(_dot(y, w_ref[:, d:2 * d]))
    k_ref[...] = k.astype(BF16)
    km_ref[0] = jnp.mean(k, axis=0, keepdims=True)
    v_ref[...] = _dot(y, w_ref[:, 2 * d:]).astype(BF16)


def _qkv_layer(h, g, w_qkv, cos2, sin2, seq):
    n, d = h.shape
    tm = MOBA_BLOCK
    tps = seq // tm
    row_spec = pl.BlockSpec((tm, d), lambda i: (i, 0))
    tab_spec = pl.BlockSpec((tm, HEAD_DIM), lambda i: (i % tps, 0))
    return pl.pallas_call(
        _qkv_kernel,
        grid=(n // tm,),
        in_specs=[row_spec, _const_spec((1, d)), _const_spec((d, 3 * d)), tab_spec, tab_spec],
        out_specs=[row_spec, row_spec, row_spec, row_spec,
                   pl.BlockSpec((1, 1, d), lambda i: (i, 0, 0))],
        out_shape=[jax.ShapeDtypeStruct((n, d), BF16)] * 4
        + [jax.ShapeDtypeStruct((n // tm, 1, d), F32)],
        compiler_params=pltpu.CompilerParams(
            dimension_semantics=("parallel",), vmem_limit_bytes=VMEM_LIMIT),
        name="qkv_rope",
    )(h, g.reshape(1, d), w_qkv.astype(BF16), cos2, sin2)


def _attn_kernel(qh_ref, ql_ref, k_ref, v_ref, km_ref, o_ref):
    seq = qh_ref.shape[0]
    nb = seq // MOBA_BLOCK
    blk = MOBA_BLOCK
    km = km_ref[:, 0, :]
    km_hi = km.astype(BF16)
    km_lo = (km - km_hi.astype(F32)).astype(BF16)
    lane = lax.broadcasted_iota(jnp.int32, (blk, nb), 1)
    qi = lax.broadcasted_iota(jnp.int32, (blk, blk), 0)
    ki = lax.broadcasted_iota(jnp.int32, (blk, blk), 1)
    causal = ki <= qi

    for i in range(nb):
        rows = slice(i * blk, (i + 1) * blk)
        qh = qh_ref[rows, :]
        n_sel = min(MOBA_TOPK, i)
        if n_sel > 0:
            ql = ql_ref[rows, :]
            gate = _dot_nt(qh, km_hi) + (_dot_nt(qh, km_lo) + _dot_nt(ql, km_hi))
            gate = jnp.where(lane < i, gate, NEG_INF)
            rank = jnp.zeros((blk, nb), jnp.int32)
            for m in range(i):
                gm = gate[:, m:m + 1]
                ahead = (gm > gate) | ((gm == gate) & (m < lane))
                rank = rank + ahead.astype(jnp.int32)
            sel = jnp.where((rank < n_sel) & (lane < i), 1.0, 0.0)

        s = jnp.where(causal, _dot_nt(qh, k_ref[rows, :]), NEG_INF)
        m_run = jnp.max(s, axis=-1, keepdims=True)
        p = jnp.exp(s - m_run)
        l_run = jnp.sum(p, axis=-1, keepdims=True)
        acc = _dot(p.astype(BF16), v_ref[rows, :])
        for n in range(i):
            krows = slice(n * blk, (n + 1) * blk)
            s = jnp.where(sel[:, n:n + 1] > 0.5, _dot_nt(qh, k_ref[krows, :]), NEG_INF)
            m_new = jnp.maximum(m_run, jnp.max(s, axis=-1, keepdims=True))
            alpha = jnp.exp(m_run - m_new)
            p = jnp.exp(s - m_new)
            l_run = alpha * l_run + jnp.sum(p, axis=-1, keepdims=True)
            acc = alpha * acc + _dot(p.astype(BF16), v_ref[krows, :])
            m_run = m_new
        o_ref[rows, :] = (acc / l_run).astype(o_ref.dtype)


def _attn_layer(qh, ql, k, v, kmean, batch, seq):
    n, d = qh.shape
    nb = seq // MOBA_BLOCK
    head_spec = pl.BlockSpec((seq, HEAD_DIM), lambda b, h: (b, h))
    return pl.pallas_call(
        _attn_kernel,
        grid=(batch, N_HEADS),
        in_specs=[head_spec, head_spec, head_spec, head_spec,
                  pl.BlockSpec((nb, 1, HEAD_DIM), lambda b, h: (b, 0, h))],
        out_specs=head_spec,
        out_shape=jax.ShapeDtypeStruct((n, d), BF16),
        compiler_params=pltpu.CompilerParams(
            dimension_semantics=("parallel", "parallel"), vmem_limit_bytes=VMEM_LIMIT),
        name="moba_attention",
    )(qh, ql, k, v, kmean)


def _proj_kernel(h_ref, a_ref, w_ref, o_ref):
    o_ref[...] = h_ref[...] + _dot(a_ref[...], w_ref[...])


def _proj_layer(h, a, w_o, tm=512):
    n, d = h.shape
    row_spec = pl.BlockSpec((tm, d), lambda i: (i, 0))
    return pl.pallas_call(
        _proj_kernel,
        grid=(n // tm,),
        in_specs=[row_spec, row_spec, _const_spec((d, d))],
        out_specs=row_spec,
        out_shape=jax.ShapeDtypeStruct((n, d), F32),
        compiler_params=pltpu.CompilerParams(
            dimension_semantics=("parallel",), vmem_limit_bytes=VMEM_LIMIT),
        name="attn_out_proj",
    )(h, a, w_o.astype(BF16))


def _rope_tables(seq):
    half = HEAD_DIM // 2
    inv = ROPE_THETA ** (-jnp.arange(half, dtype=F32) / half)
    ang = jnp.arange(seq).astype(F32)[:, None] * inv[None, :]
    cos, sin = jnp.cos(ang), jnp.sin(ang)
    return jnp.concatenate([cos, cos], axis=-1), jnp.concatenate([-sin, sin], axis=-1)


def kernel(x, mix_norm, a_w_in, a_v_gain, a_w_s, a_b_s, a_w_out, b_w_qkv, b_w_o, ffn_norm,
           ffn_w_up, ffn_conv_w, ffn_conv_b, ffn_w_down, final_norm):
    batch, seq, d = x.shape
    assert seq % MOBA_BLOCK == 0 and d == N_HEADS * HEAD_DIM
    h = x.reshape(batch * seq, d)

    h = _gmlp_layer(h, mix_norm[0], a_w_in[0], a_v_gain[0], a_w_s[0], a_b_s[0], a_w_out[0])
    h = _ffn_layer(h, ffn_norm[0], ffn_w_up[0], ffn_conv_w[0], ffn_conv_b[0], ffn_w_down[0],
                   final_norm, seq, final_norm=False)

    cos2, sin2 = _rope_tables(seq)
    qh, ql, k, v, kmean = _qkv_layer(h, mix_norm[1], b_w_qkv[0], cos2, sin2, seq)
    attn = _attn_layer(qh, ql, k, v, kmean, batch, seq)
    h = _proj_layer(h, attn, b_w_o[0])
    h = _ffn_layer(h, ffn_norm[1], ffn_w_up[1], ffn_conv_w[1], ffn_conv_b[1], ffn_w_down[1],
                   final_norm, seq, final_norm=True)
    return h.reshape(batch, seq, d)
```

```python
import functools
import math

import jax
import jax.numpy as jnp
from jax import lax
from jax.experimental import pallas as pl
from jax.experimental.pallas import tpu as pltpu

F32 = jnp.float32
BF16 = jnp.bfloat16

EPS = 1e-6
NEG_INF = -1e30
GMLP_CHUNK = 128
GMLP_GROUPS = 8
N_HEADS = 8
HEAD_DIM = 128
MOBA_BLOCK = 256
MOBA_TOPK = 3
ROPE_THETA = 10000.0
CONV_WIDTH = 3
SUBLANES = 8
VMEM_LIMIT = 56 * 1024 * 1024


def _rms(x, g):
    return x * lax.rsqrt(jnp.mean(x * x, axis=-1, keepdims=True) + EPS) * g


def _gelu(x):
    c = math.sqrt(2.0 / math.pi)
    return x * (0.5 * (1.0 + jnp.tanh(c * (x + 0.044715 * (x * x * x)))))


def _dot(a, b):
    return jnp.dot(a, b, preferred_element_type=F32)


def _dot_nt(a, b):
    return lax.dot_general(a, b, (((1,), (1,)), ((), ())), preferred_element_type=F32)


def _const_spec(shape):
    return pl.BlockSpec(shape, lambda *_: (0,) * len(shape), pipeline_mode=pl.Buffered(1))


def _gmlp_kernel(h_ref, g_ref, win_ref, vg_ref, ws_ref, bst_ref, wout_ref, o_ref, v_scr):
    tm = h_ref.shape[0]
    width = vg_ref.shape[1]
    gdim = width // GMLP_GROUPS
    h = h_ref[...]
    y = _rms(h, g_ref[...]).astype(BF16)
    v = _gelu(_dot(y, win_ref[:, width:]))
    v = _rms(v, vg_ref[...])
    v_scr[...] = v.astype(BF16)
    t_idx = lax.broadcasted_iota(jnp.int32, (GMLP_CHUNK, GMLP_CHUNK), 0)
    s_idx = lax.broadcasted_iota(jnp.int32, (GMLP_CHUNK, GMLP_CHUNK), 1)
    causal = s_idx <= t_idx
    acc = h
    for g in range(GMLP_GROUPS):
        cols = slice(g * gdim, (g + 1) * gdim)
        u = _gelu(_dot(y, win_ref[:, cols]))
        ws = jnp.where(causal, ws_ref[g], 0.0).astype(BF16)
        bias = bst_ref[:, g:g + 1]
        s = jnp.concatenate(
            [_dot(ws, v_scr[c * GMLP_CHUNK:(c + 1) * GMLP_CHUNK, cols]) + bias
             for c in range(tm // GMLP_CHUNK)], axis=0)
        acc = acc + _dot((u * s).astype(BF16), wout_ref[cols, :])
    o_ref[...] = acc


def _gmlp_layer(h, g, w_in, v_gain, w_s, b_s, w_out, tm=256):
    n, d = h.shape
    width = v_gain.shape[-1]
    return pl.pallas_call(
        _gmlp_kernel,
        grid=(n // tm,),
        in_specs=[
            pl.BlockSpec((tm, d), lambda i: (i, 0)),
            _const_spec((1, d)),
            _const_spec((d, 2 * width)),
            _const_spec((1, width)),
            _const_spec(w_s.shape),
            _const_spec((GMLP_CHUNK, GMLP_GROUPS)),
            _const_spec((width, d)),
        ],
        out_specs=pl.BlockSpec((tm, d), lambda i: (i, 0)),
        out_shape=jax.ShapeDtypeStruct((n, d), F32),
        scratch_shapes=[pltpu.VMEM((tm, width), BF16)],
        compiler_params=pltpu.CompilerParams(
            dimension_semantics=("parallel",), vmem_limit_bytes=VMEM_LIMIT),
        name="gmlp_mixer",
    )(h, g.reshape(1, d), w_in.astype(BF16), v_gain.reshape(1, width), w_s, b_s.T,
      w_out.astype(BF16))


def _ffn_kernel(h_ref, g_ref, wup_ref, cw_ref, cb_ref, wdown_ref, fg_ref, o_ref, carry_ref,
                *, tiles_per_seq, fc, final_norm):
    tm = h_ref.shape[0]
    d_ff = wdown_ref.shape[0]

    @pl.when(pl.program_id(0) % tiles_per_seq == 0)
    def _():
        carry_ref[...] = jnp.zeros_like(carry_ref)

    h = h_ref[...]
    y = _rms(h, g_ref[...]).astype(BF16)
    row = lax.broadcasted_iota(jnp.int32, (tm, fc), 0)
    acc = h
    for j in range(d_ff // fc):
        conv = []
        for half in range(2):
            cols = slice(half * d_ff + j * fc, half * d_ff + (j + 1) * fc)
            a = _dot(y, wup_ref[:, cols])
            prev = carry_ref[:, cols]
            carry_ref[:, cols] = a[tm - SUBLANES:, :]
            a1 = jnp.where(row == 0, prev[SUBLANES - 1:SUBLANES, :], pltpu.roll(a, 1, 0))
            a2 = jnp.where(row == 0, prev[SUBLANES - 2:SUBLANES - 1, :],
                           jnp.where(row == 1, prev[SUBLANES - 1:SUBLANES, :], pltpu.roll(a, 2, 0)))
            conv.append(cw_ref[0:1, cols] * a2 + cw_ref[1:2, cols] * a1 + cw_ref[2:3, cols] * a
                        + cb_ref[:, cols])
        act = (_gelu(conv[0]) * conv[1]).astype(BF16)
        acc = acc + _dot(act, wdown_ref[j * fc:(j + 1) * fc, :])
    if final_norm:
        acc = _rms(acc, fg_ref[...])
    o_ref[...] = acc


def _ffn_layer(h, g, w_up, conv_w, conv_b, w_down, final_g, seq, final_norm, tm=512, fc=256):
    n, d = h.shape
    d_ff = w_down.shape[0]
    kern = functools.partial(_ffn_kernel, tiles_per_seq=seq // tm, fc=fc, final_norm=final_norm)
    return pl.pallas_call(
        kern,
        grid=(n // tm,),
        in_specs=[
            pl.BlockSpec((tm, d), lambda i: (i, 0)),
            _const_spec((1, d)),
            _const_spec((d, 2 * d_ff)),
            _const_spec((CONV_WIDTH, 2 * d_ff)),
            _const_spec((1, 2 * d_ff)),
            _const_spec((d_ff, d)),
            _const_spec((1, d)),
        ],
        out_specs=pl.BlockSpec((tm, d), lambda i: (i, 0)),
        out_shape=jax.ShapeDtypeStruct((n, d), F32),
        scratch_shapes=[pltpu.VMEM((SUBLANES, 2 * d_ff), F32)],
        compiler_params=pltpu.CompilerParams(
            dimension_semantics=("arbitrary",), vmem_limit_bytes=VMEM_LIMIT),
        name="conv_ffn",
    )(h, g.reshape(1, d), w_up.astype(BF16), conv_w, conv_b.reshape(1, 2 * d_ff),
      w_down.astype(BF16), final_g.reshape(1, d))


def _qkv_kernel(h_ref, g_ref, w_ref, cos_ref, sin_ref, qh_ref, ql_ref, k_ref, vt_ref, km_ref):
    d = h_ref.shape[1]
    y = _rms(h_ref[...], g_ref[...]).astype(BF16)
    cos = cos_ref[...]
    sin = sin_ref[...]

    def rope(x):
        heads = []
        for hh in range(N_HEADS):
            xh = x[:, hh * HEAD_DIM:(hh + 1) * HEAD_DIM]
            heads.append(xh * cos + pltpu.roll(xh, HEAD_DIM // 2, 1) * sin)
        return jnp.concatenate(heads, axis=1)

    q = rope(_dot(y, w_ref[:, :d])) * (HEAD_DIM ** -0.5 * math.log2(math.e))
    qh = q.astype(BF16)
    qh_ref[...] = qh
    ql_ref[...] = (q - qh.astype(F32)).astype(BF16)
    k = rope(_dot(y, w_ref[:, d:2 * d]))
    k_ref[...] = k.astype(BF16)
    km_ref[0] = jnp.mean(k, axis=0, keepdims=True)
    vt_ref[...] = _dot(y, w_ref[:, 2 * d:]).T.astype(BF16)


def _qkv_layer(h, g, w_qkv, cos2, sin2, seq):
    n, d = h.shape
    tm = MOBA_BLOCK
    tps = seq // tm
    row_spec = pl.BlockSpec((tm, d), lambda i: (i, 0))
    tab_spec = pl.BlockSpec((tm, HEAD_DIM), lambda i: (i % tps, 0))
    return pl.pallas_call(
        _qkv_kernel,
        grid=(n // tm,),
        in_specs=[row_spec, _const_spec((1, d)), _const_spec((d, 3 * d)), tab_spec, tab_spec],
        out_specs=[row_spec, row_spec, row_spec,
                   pl.BlockSpec((d, tm), lambda i: (i // tps, i % tps)),
                   pl.BlockSpec((1, 1, d), lambda i: (i, 0, 0))],
        out_shape=[jax.ShapeDtypeStruct((n, d), BF16)] * 3
        + [jax.ShapeDtypeStruct((n // seq * d, seq), BF16),
           jax.ShapeDtypeStruct((n // tm, 1, d), F32)],
        compiler_params=pltpu.CompilerParams(
            dimension_semantics=("parallel",), vmem_limit_bytes=VMEM_LIMIT),
        name="qkv_rope",
    )(h, g.reshape(1, d), w_qkv.astype(BF16), cos2, sin2)


def _attn_kernel(qh_ref, ql_ref, k_ref, vt_ref, km_ref, o_ref):
    seq = qh_ref.shape[0]
    nb = seq // MOBA_BLOCK
    blk = MOBA_BLOCK
    km = km_ref[:, 0, :]
    km_hi = km.astype(BF16).astype(F32)
    km2 = jnp.concatenate([km_hi, km - km_hi], axis=0).astype(BF16)
    blk_id = lax.broadcasted_iota(jnp.int32, (nb, blk), 0)
    key_i = lax.broadcasted_iota(jnp.int32, (blk, blk), 0)
    qry_i = lax.broadcasted_iota(jnp.int32, (blk, blk), 1)
    causal = key_i <= qry_i

    for i in range(nb):
        rows = slice(i * blk, (i + 1) * blk)
        qh = qh_ref[rows, :]
        s = _dot_nt(k_ref[0:(i + 1) * blk, :], qh)
        s_own = jnp.where(causal, s[i * blk:, :], NEG_INF)
        m = jnp.max(s_own, axis=0, keepdims=True)
        n_sel = min(MOBA_TOPK, i)
        if n_sel > 0:
            g_hi = _dot_nt(km2, qh)
            g_lo = _dot_nt(km2, ql_ref[rows, :])
            gate = g_hi[:nb, :] + (g_hi[nb:, :] + g_lo[:nb, :])
            gate = jnp.where(blk_id < i, gate, NEG_INF)
            rank = jnp.zeros((nb, blk), jnp.int32)
            for mm in range(i):
                gm = gate[mm:mm + 1, :]
                ahead = (gm > gate) | ((gm == gate) & (mm < blk_id))
                rank = rank + ahead.astype(jnp.int32)
            bias = jnp.where((rank < n_sel) & (blk_id < i), 0.0, NEG_INF)
            for n in range(i):
                bmax = jnp.max(s[n * blk:(n + 1) * blk, :], axis=0, keepdims=True)
                m = jnp.maximum(m, bmax + bias[n:n + 1, :])
        parts = []
        l_sum = jnp.zeros((1, blk), F32)
        for n in range(i):
            p = jnp.exp2(s[n * blk:(n + 1) * blk, :] + (bias[n:n + 1, :] - m))
            l_sum = l_sum + jnp.sum(p, axis=0, keepdims=True)
            parts.append(p.astype(BF16))
        p = jnp.exp2(s_own - m)
        l_sum = l_sum + jnp.sum(p, axis=0, keepdims=True)
        parts.append(p.astype(BF16))
        acc = _dot(vt_ref[:, 0:(i + 1) * blk], jnp.concatenate(parts, axis=0))
        o_ref[rows, :] = (acc * (1.0 / l_sum)).T.astype(o_ref.dtype)


def _attn_layer(qh, ql, k, vt, kmean, batch, seq):
    n, d = qh.shape
    nb = seq // MOBA_BLOCK
    head_spec = pl.BlockSpec((seq, HEAD_DIM), lambda b, h: (b, h))
    return pl.pallas_call(
        _attn_kernel,
        grid=(batch, N_HEADS),
        in_specs=[head_spec, head_spec, head_spec,
                  pl.BlockSpec((HEAD_DIM, seq), lambda b, h: (b * N_HEADS + h, 0)),
                  pl.BlockSpec((nb, 1, HEAD_DIM), lambda b, h: (b, 0, h))],
        out_specs=head_spec,
        out_shape=jax.ShapeDtypeStruct((n, d), BF16),
        compiler_params=pltpu.CompilerParams(
            dimension_semantics=("parallel", "parallel"), vmem_limit_bytes=VMEM_LIMIT),
        name="moba_attention",
    )(qh, ql, k, vt, kmean)


def _proj_kernel(h_ref, a_ref, w_ref, o_ref):
    o_ref[...] = h_ref[...] + _dot(a_ref[...], w_ref[...])


def _proj_layer(h, a, w_o, tm=512):
    n, d = h.shape
    row_spec = pl.BlockSpec((tm, d), lambda i: (i, 0))
    return pl.pallas_call(
        _proj_kernel,
        grid=(n // tm,),
        in_specs=[row_spec, row_spec, _const_spec((d, d))],
        out_specs=row_spec,
        out_shape=jax.ShapeDtypeStruct((n, d), F32),
        compiler_params=pltpu.CompilerParams(
            dimension_semantics=("parallel",), vmem_limit_bytes=VMEM_LIMIT),
        name="attn_out_proj",
    )(h, a, w_o.astype(BF16))


def _rope_tables(seq):
    half = HEAD_DIM // 2
    inv = ROPE_THETA ** (-jnp.arange(half, dtype=F32) / half)
    ang = jnp.arange(seq).astype(F32)[:, None] * inv[None, :]
    cos, sin = jnp.cos(ang), jnp.sin(ang)
    return jnp.concatenate([cos, cos], axis=-1), jnp.concatenate([-sin, sin], axis=-1)


def kernel(x, mix_norm, a_w_in, a_v_gain, a_w_s, a_b_s, a_w_out, b_w_qkv, b_w_o, ffn_norm,
           ffn_w_up, ffn_conv_w, ffn_conv_b, ffn_w_down, final_norm):
    batch, seq, d = x.shape
    assert seq % MOBA_BLOCK == 0 and d == N_HEADS * HEAD_DIM
    h = x.reshape(batch * seq, d)

    h = _gmlp_layer(h, mix_norm[0], a_w_in[0], a_v_gain[0], a_w_s[0], a_b_s[0], a_w_out[0])
    h = _ffn_layer(h, ffn_norm[0], ffn_w_up[0], ffn_conv_w[0], ffn_conv_b[0], ffn_w_down[0],
                   final_norm, seq, final_norm=False)

    cos2, sin2 = _rope_tables(seq)
    qh, ql, k, vt, kmean = _qkv_layer(h, mix_norm[1], b_w_qkv[0], cos2, sin2, seq)
    attn = _attn_layer(qh, ql, k, vt, kmean, batch, seq)
    h = _proj_layer(h, attn, b_w_o[0])
    h = _ffn_layer(h, ffn_norm[1], ffn_w_up[1], ffn_conv_w[1], ffn_conv_b[1], ffn_w_down[1],
                   final_norm, seq, final_norm=True)
    return h.reshape(batch, seq, d)
```

```python
import functools
import math

import jax
import jax.numpy as jnp
from jax import lax
from jax.experimental import pallas as pl
from jax.experimental.pallas import tpu as pltpu

F32 = jnp.float32
BF16 = jnp.bfloat16

EPS = 1e-6
NEG_INF = -1e30
GMLP_CHUNK = 128
GMLP_GROUPS = 8
N_HEADS = 8
HEAD_DIM = 128
MOBA_BLOCK = 256
MOBA_TOPK = 3
ROPE_THETA = 10000.0
CONV_WIDTH = 3
SUBLANES = 8
LANES = 128
VMEM_LIMIT = 56 * 1024 * 1024


def _rms(x, g):
    return x * lax.rsqrt(jnp.mean(x * x, axis=-1, keepdims=True) + EPS) * g


def _gelu(x):
    c = math.sqrt(2.0 / math.pi)
    return x * (0.5 * (1.0 + jnp.tanh(c * (x + 0.044715 * (x * x * x)))))


def _dot(a, b):
    return jnp.dot(a, b, preferred_element_type=F32)


def _dot_nt(a, b):
    return lax.dot_general(a, b, (((1,), (1,)), ((), ())), preferred_element_type=F32)


def _const_spec(shape):
    return pl.BlockSpec(shape, lambda *_: (0,) * len(shape), pipeline_mode=pl.Buffered(1))


def _gmlp_kernel(h_ref, g_ref, win_ref, vg_ref, ws_ref, bst_ref, wout_ref, o_ref, v_scr):
    tm = h_ref.shape[0]
    width = vg_ref.shape[1]
    gdim = width // GMLP_GROUPS
    h = h_ref[...]
    y = _rms(h, g_ref[...]).astype(BF16)
    v = _gelu(_dot(y, win_ref[:, width:]))
    v = _rms(v, vg_ref[...])
    v_scr[...] = v.astype(BF16)
    t_idx = lax.broadcasted_iota(jnp.int32, (GMLP_CHUNK, GMLP_CHUNK), 0)
    s_idx = lax.broadcasted_iota(jnp.int32, (GMLP_CHUNK, GMLP_CHUNK), 1)
    causal = s_idx <= t_idx

    def group_cols(g):
        return slice(g * gdim, (g + 1) * gdim)

    def group_dots(g):
        cols = group_cols(g)
        ws = jnp.where(causal, ws_ref[g], 0.0).astype(BF16)
        mixed = [_dot(ws, v_scr[c * GMLP_CHUNK:(c + 1) * GMLP_CHUNK, cols])
                 for c in range(tm // GMLP_CHUNK)]
        return _dot(y, win_ref[:, cols]), mixed

    acc = h
    nxt = group_dots(0)
    for g in range(GMLP_GROUPS):
        u_pre, mixed = nxt
        if g + 1 < GMLP_GROUPS:
            nxt = group_dots(g + 1)
        bias = bst_ref[:, g:g + 1]
        s = jnp.concatenate([mx + bias for mx in mixed], axis=0)
        acc = acc + _dot((_gelu(u_pre) * s).astype(BF16), wout_ref[group_cols(g), :])
    o_ref[...] = acc


def _gmlp_layer(h, g, w_in, v_gain, w_s, b_s, w_out, tm=512):
    n, d = h.shape
    width = v_gain.shape[-1]
    return pl.pallas_call(
        _gmlp_kernel,
        grid=(n // tm,),
        in_specs=[
            pl.BlockSpec((tm, d), lambda i: (i, 0)),
            _const_spec((1, d)),
            _const_spec((d, 2 * width)),
            _const_spec((1, width)),
            _const_spec(w_s.shape),
            _const_spec((GMLP_CHUNK, GMLP_GROUPS)),
            _const_spec((width, d)),
        ],
        out_specs=pl.BlockSpec((tm, d), lambda i: (i, 0)),
        out_shape=jax.ShapeDtypeStruct((n, d), F32),
        scratch_shapes=[pltpu.VMEM((tm, width), BF16)],
        compiler_params=pltpu.CompilerParams(
            dimension_semantics=("parallel",), vmem_limit_bytes=VMEM_LIMIT),
        name="gmlp_mixer",
    )(h, g.reshape(1, d), w_in.astype(BF16), v_gain.reshape(1, width), w_s, b_s.T,
      w_out.astype(BF16))


def _ffn_kernel(h_ref, g_ref, wup_ref, cw_ref, cb_ref, wdown_ref, fg_ref, o_ref,
                carry_ref, perm_in, perm_out, *, tiles_per_seq, fc, final_norm):
    tm = h_ref.shape[0]
    d_ff = wdown_ref.shape[0]
    seg = tm // SUBLANES
    n_chunks = d_ff // fc

    @pl.when(pl.program_id(0) % tiles_per_seq == 0)
    def _():
        carry_ref[...] = jnp.zeros_like(carry_ref)

    n_lane_blocks = h_ref.shape[1] // LANES
    y_true = _rms(h_ref[...], g_ref[...])
    for c in range(n_lane_blocks):
        perm_in[c] = y_true[:, c * LANES:(c + 1) * LANES]
    y = jnp.concatenate(
        [jnp.concatenate([perm_in[c, pl.ds(v, SUBLANES, stride=seg), :] for c in range(n_lane_blocks)],
                         axis=1) for v in range(seg)], axis=0).astype(BF16)
    first_sublane = lax.broadcasted_iota(jnp.int32, (SUBLANES, fc), 0) == 0

    def cols_of(j, half):
        return slice(half * d_ff + j * fc, half * d_ff + (j + 1) * fc)

    def up_proj(j):
        return [_dot(y, wup_ref[:, cols_of(j, half)]) for half in range(2)]

    def conv_act(j, a_pair):
        conv = []
        for half in range(2):
            cols = cols_of(j, half)
            a = a_pair[half]
            prev = carry_ref[:, cols]
            carry_ref[:, cols] = a[tm - 2 * SUBLANES:, :]
            wrap = [jnp.where(first_sublane,
                              pltpu.roll(prev[r * SUBLANES:(r + 1) * SUBLANES, :], 1, 0),
                              pltpu.roll(a[tm - (2 - r) * SUBLANES:tm - (1 - r) * SUBLANES, :], 1, 0))
                    for r in range(2)]
            a1 = jnp.concatenate([wrap[1], a[:tm - SUBLANES, :]], axis=0)
            a2 = jnp.concatenate([wrap[0], wrap[1], a[:tm - 2 * SUBLANES, :]], axis=0)
            conv.append(cw_ref[0:1, cols] * a2 + cw_ref[1:2, cols] * a1 + cw_ref[2:3, cols] * a
                        + cb_ref[:, cols])
        return (_gelu(conv[0]) * conv[1]).astype(BF16)

    acc = None
    a_next = up_proj(0)
    for j in range(n_chunks):
        a_cur = a_next
        if j + 1 < n_chunks:
            a_next = up_proj(j + 1)
        act = conv_act(j, a_cur)
        down = _dot(act, wdown_ref[j * fc:(j + 1) * fc, :])
        acc = down if acc is None else acc + down
    for v in range(seg):
        for c in range(n_lane_blocks):
            perm_out[c, pl.ds(v, SUBLANES, stride=seg), :] = (
                acc[v * SUBLANES:(v + 1) * SUBLANES, c * LANES:(c + 1) * LANES])
    out = h_ref[...] + jnp.concatenate([perm_out[c] for c in range(n_lane_blocks)], axis=1)
    if final_norm:
        out = _rms(out, fg_ref[...])
    o_ref[...] = out


def _ffn_layer(h, g, w_up, conv_w, conv_b, w_down, final_g, seq, final_norm, tm=512, fc=256):
    n, d = h.shape
    d_ff = w_down.shape[0]
    kern = functools.partial(_ffn_kernel, tiles_per_seq=seq // tm, fc=fc, final_norm=final_norm)
    return pl.pallas_call(
        kern,
        grid=(n // tm,),
        in_specs=[
            pl.BlockSpec((tm, d), lambda i: (i, 0)),
            _const_spec((1, d)),
            _const_spec((d, 2 * d_ff)),
            _const_spec((CONV_WIDTH, 2 * d_ff)),
            _const_spec((1, 2 * d_ff)),
            _const_spec((d_ff, d)),
            _const_spec((1, d)),
        ],
        out_specs=pl.BlockSpec((tm, d), lambda i: (i, 0)),
        out_shape=jax.ShapeDtypeStruct((n, d), F32),
        scratch_shapes=[pltpu.VMEM((2 * SUBLANES, 2 * d_ff), F32),
                        pltpu.VMEM((d // LANES, tm, LANES), F32),
                        pltpu.VMEM((d // LANES, tm, LANES), F32)],
        compiler_params=pltpu.CompilerParams(
            dimension_semantics=("arbitrary",), vmem_limit_bytes=VMEM_LIMIT),
        name="conv_ffn",
    )(h, g.reshape(1, d), w_up.astype(BF16), conv_w, conv_b.reshape(1, 2 * d_ff),
      w_down.astype(BF16), final_g.reshape(1, d))


def _qkv_kernel(h_ref, g_ref, w_ref, cos_ref, sin_ref, qh_ref, ql_ref, k_ref, vt_ref, km_ref):
    d = h_ref.shape[1]
    y = _rms(h_ref[...], g_ref[...]).astype(BF16)
    cos = cos_ref[...]
    sin = sin_ref[...]

    def rope(x):
        heads = []
        for hh in range(N_HEADS):
            xh = x[:, hh * HEAD_DIM:(hh + 1) * HEAD_DIM]
            heads.append(xh * cos + pltpu.roll(xh, HEAD_DIM // 2, 1) * sin)
        return jnp.concatenate(heads, axis=1)

    q = rope(_dot(y, w_ref[:, :d])) * (HEAD_DIM ** -0.5 * math.log2(math.e))
    qh = q.astype(BF16)
    qh_ref[...] = qh
    ql_ref[...] = (q - qh.astype(F32)).astype(BF16)
    k = rope(_dot(y, w_ref[:, d:2 * d]))
    k_ref[...] = k.astype(BF16)
    km_ref[0] = jnp.mean(k, axis=0, keepdims=True)
    vt_ref[...] = _dot(y, w_ref[:, 2 * d:]).T.astype(BF16)


def _qkv_layer(h, g, w_qkv, cos2, sin2, seq):
    n, d = h.shape
    tm = MOBA_BLOCK
    tps = seq // tm
    row_spec = pl.BlockSpec((tm, d), lambda i: (i, 0))
    tab_spec = pl.BlockSpec((tm, HEAD_DIM), lambda i: (i % tps, 0))
    return pl.pallas_call(
        _qkv_kernel,
        grid=(n // tm,),
        in_specs=[row_spec, _const_spec((1, d)), _const_spec((d, 3 * d)), tab_spec, tab_spec],
        out_specs=[row_spec, row_spec, row_spec,
                   pl.BlockSpec((d, tm), lambda i: (i // tps, i % tps)),
                   pl.BlockSpec((1, 1, d), lambda i: (i, 0, 0))],
        out_shape=[jax.ShapeDtypeStruct((n, d), BF16)] * 3
        + [jax.ShapeDtypeStruct((n // seq * d, seq), BF16),
           jax.ShapeDtypeStruct((n // tm, 1, d), F32)],
        compiler_params=pltpu.CompilerParams(
            dimension_semantics=("parallel",), vmem_limit_bytes=VMEM_LIMIT),
        name="qkv_rope",
    )(h, g.reshape(1, d), w_qkv.astype(BF16), cos2, sin2)


def _attn_kernel(qh_ref, ql_ref, k_ref, vt_ref, km_ref, o_ref):
    seq = qh_ref.shape[0]
    nb = seq // MOBA_BLOCK
    blk = MOBA_BLOCK
    km = km_ref[:, 0, :]
    km_hi = km.astype(BF16).astype(F32)
    km2 = jnp.concatenate([km_hi, km - km_hi], axis=0).astype(BF16)
    blk_id = lax.broadcasted_iota(jnp.int32, (nb, blk), 0)
    key_i = lax.broadcasted_iota(jnp.int32, (blk, blk), 0)
    qry_i = lax.broadcasted_iota(jnp.int32, (blk, blk), 1)
    causal = key_i <= qry_i

    def block(i):
        return slice(i * blk, (i + 1) * blk)

    def scores(i, n):
        return _dot_nt(k_ref[block(n), :], qh_ref[block(i), :])

    def gate_dots(i):
        return _dot_nt(km2, qh_ref[block(i), :]), _dot_nt(km2, ql_ref[block(i), :])

    def softmax_shift(i, s, g):
        s_own = jnp.where(causal, s[i], NEG_INF)
        m = jnp.max(s_own, axis=0, keepdims=True)
        n_sel = min(MOBA_TOPK, i)
        if n_sel == 0:
            return s_own, [-m]
        g_hi, g_lo = g
        gate = g_hi[:nb, :] + (g_hi[nb:, :] + g_lo[:nb, :])
        gate = jnp.where(blk_id < i, gate, NEG_INF)
        rank = jnp.zeros((nb, blk), jnp.int32)
        for mm in range(i):
            gm = gate[mm:mm + 1, :]
            ahead = (gm > gate) | ((gm == gate) & (mm < blk_id))
            rank = rank + ahead.astype(jnp.int32)
        bias = jnp.where((rank < n_sel) & (blk_id < i), 0.0, NEG_INF)
        for n in range(i):
            m = jnp.maximum(m, jnp.max(s[n], axis=0, keepdims=True) + bias[n:n + 1, :])
        return s_own, [bias[n:n + 1, :] - m for n in range(i)] + [-m]

    s_cur, g_cur = [scores(0, 0)], None
    for i in range(nb):
        s_own, shift = softmax_shift(i, s_cur, g_cur)
        s_cur[i] = s_own
        s_next, g_next = [], None
        acc, l_sum = None, None
        for n in range(i + 2):
            if i + 1 < nb:
                s_next.append(scores(i + 1, n))
                if n == 0:
                    g_next = gate_dots(i + 1)
            if n <= i:
                p = jnp.exp2(s_cur[n] + shift[n])
                l_part = jnp.sum(p, axis=0, keepdims=True)
                pv = _dot(vt_ref[:, block(n)], p.astype(BF16))
                acc = pv if acc is None else acc + pv
                l_sum = l_part if l_sum is None else l_sum + l_part
        o_ref[block(i), :] = (acc * (1.0 / l_sum)).T.astype(o_ref.dtype)
        s_cur, g_cur = s_next, g_next


def _attn_layer(qh, ql, k, vt, kmean, batch, seq):
    n, d = qh.shape
    nb = seq // MOBA_BLOCK
    head_spec = pl.BlockSpec((seq, HEAD_DIM), lambda b, h: (b, h))
    return pl.pallas_call(
        _attn_kernel,
        grid=(batch, N_HEADS),
        in_specs=[head_spec, head_spec, head_spec,
                  pl.BlockSpec((HEAD_DIM, seq), lambda b, h: (b * N_HEADS + h, 0)),
                  pl.BlockSpec((nb, 1, HEAD_DIM), lambda b, h: (b, 0, h))],
        out_specs=head_spec,
        out_shape=jax.ShapeDtypeStruct((n, d), BF16),
        compiler_params=pltpu.CompilerParams(
            dimension_semantics=("parallel", "parallel"), vmem_limit_bytes=VMEM_LIMIT),
        name="moba_attention",
    )(qh, ql, k, vt, kmean)


def _proj_kernel(h_ref, a_ref, w_ref, o_ref):
    o_ref[...] = h_ref[...] + _dot(a_ref[...], w_ref[...])


def _proj_layer(h, a, w_o, tm=512):
    n, d = h.shape
    row_spec = pl.BlockSpec((tm, d), lambda i: (i, 0))
    return pl.pallas_call(
        _proj_kernel,
        grid=(n // tm,),
        in_specs=[row_spec, row_spec, _const_spec((d, d))],
        out_specs=row_spec,
        out_shape=jax.ShapeDtypeStruct((n, d), F32),
        compiler_params=pltpu.CompilerParams(
            dimension_semantics=("parallel",), vmem_limit_bytes=VMEM_LIMIT),
        name="attn_out_proj",
    )(h, a, w_o.astype(BF16))


def _rope_tables(seq):
    half = HEAD_DIM // 2
    inv = ROPE_THETA ** (-jnp.arange(half, dtype=F32) / half)
    ang = jnp.arange(seq).astype(F32)[:, None] * inv[None, :]
    cos, sin = jnp.cos(ang), jnp.sin(ang)
    return jnp.concatenate([cos, cos], axis=-1), jnp.concatenate([-sin, sin], axis=-1)


def kernel(x, mix_norm, a_w_in, a_v_gain, a_w_s, a_b_s, a_w_out, b_w_qkv, b_w_o, ffn_norm,
           ffn_w_up, ffn_conv_w, ffn_conv_b, ffn_w_down, final_norm):
    batch, seq, d = x.shape
    assert seq % MOBA_BLOCK == 0 and d == N_HEADS * HEAD_DIM
    h = x.reshape(batch * seq, d)

    h = _gmlp_layer(h, mix_norm[0], a_w_in[0], a_v_gain[0], a_w_s[0], a_b_s[0], a_w_out[0])
    h = _ffn_layer(h, ffn_norm[0], ffn_w_up[0], ffn_conv_w[0], ffn_conv_b[0], ffn_w_down[0],
                   final_norm, seq, final_norm=False)

    cos2, sin2 = _rope_tables(seq)
    qh, ql, k, vt, kmean = _qkv_layer(h, mix_norm[1], b_w_qkv[0], cos2, sin2, seq)
    attn = _attn_layer(qh, ql, k, vt, kmean, batch, seq)
    h = _proj_layer(h, attn, b_w_o[0])
    h = _ffn_layer(h, ffn_norm[1], ffn_w_up[1], ffn_conv_w[1], ffn_conv_b[1], ffn_w_down[1],
                   final_norm, seq, final_norm=True)
    return h.reshape(batch, seq, d)
```

```python
import functools
import math

import jax
import jax.numpy as jnp
from jax import lax
from jax.experimental import pallas as pl
from jax.experimental.pallas import tpu as pltpu

F32 = jnp.float32
BF16 = jnp.bfloat16

EPS = 1e-6
NEG_INF = -1e30
GMLP_CHUNK = 128
GMLP_GROUPS = 8
N_HEADS = 8
HEAD_DIM = 128
MOBA_BLOCK = 256
MOBA_TOPK = 3
ROPE_THETA = 10000.0
CONV_WIDTH = 3
SUBLANES = 8
VMEM_LIMIT = 56 * 1024 * 1024


def _rms(x, g):
    return x * lax.rsqrt(jnp.mean(x * x, axis=-1, keepdims=True) + EPS) * g


def _gelu(x):
    c = math.sqrt(2.0 / math.pi)
    return x * (0.5 * (1.0 + jnp.tanh(c * (x + 0.044715 * (x * x * x)))))


def _dot(a, b):
    return jnp.dot(a, b, preferred_element_type=F32)


def _dot_nt(a, b):
    return lax.dot_general(a, b, (((1,), (1,)), ((), ())), preferred_element_type=F32)


def _const_spec(shape):
    return pl.BlockSpec(shape, lambda *_: (0,) * len(shape), pipeline_mode=pl.Buffered(1))


def _gmlp_kernel(h_ref, g_ref, win_ref, vg_ref, ws_ref, bst_ref, wout_ref, o_ref, v_scr):
    tm = h_ref.shape[0]
    width = vg_ref.shape[1]
    gdim = width // GMLP_GROUPS
    h = h_ref[...]
    y = _rms(h, g_ref[...]).astype(BF16)
    v = _gelu(_dot(y, win_ref[:, width:]))
    v = _rms(v, vg_ref[...])
    v_scr[...] = v.astype(BF16)
    t_idx = lax.broadcasted_iota(jnp.int32, (GMLP_CHUNK, GMLP_CHUNK), 0)
    s_idx = lax.broadcasted_iota(jnp.int32, (GMLP_CHUNK, GMLP_CHUNK), 1)
    causal = s_idx <= t_idx

    def group_cols(g):
        return slice(g * gdim, (g + 1) * gdim)

    def group_dots(g):
        cols = group_cols(g)
        ws = jnp.where(causal, ws_ref[g], 0.0).astype(BF16)
        mixed = [_dot(ws, v_scr[c * GMLP_CHUNK:(c + 1) * GMLP_CHUNK, cols])
                 for c in range(tm // GMLP_CHUNK)]
        return _dot(y, win_ref[:, cols]), mixed

    acc = h
    nxt = group_dots(0)
    for g in range(GMLP_GROUPS):
        u_pre, mixed = nxt
        if g + 1 < GMLP_GROUPS:
            nxt = group_dots(g + 1)
        bias = bst_ref[:, g:g + 1]
        s = jnp.concatenate([mx + bias for mx in mixed], axis=0)
        acc = acc + _dot((_gelu(u_pre) * s).astype(BF16), wout_ref[group_cols(g), :])
    o_ref[...] = acc


def _gmlp_layer(h, g, w_in, v_gain, w_s, b_s, w_out, tm=512):
    n, d = h.shape
    width = v_gain.shape[-1]
    return pl.pallas_call(
        _gmlp_kernel,
        grid=(n // tm,),
        in_specs=[
            pl.BlockSpec((tm, d), lambda i: (i, 0)),
            _const_spec((1, d)),
            _const_spec((d, 2 * width)),
            _const_spec((1, width)),
            _const_spec(w_s.shape),
            _const_spec((GMLP_CHUNK, GMLP_GROUPS)),
            _const_spec((width, d)),
        ],
        out_specs=pl.BlockSpec((tm, d), lambda i: (i, 0)),
        out_shape=jax.ShapeDtypeStruct((n, d), F32),
        scratch_shapes=[pltpu.VMEM((tm, width), BF16)],
        compiler_params=pltpu.CompilerParams(
            dimension_semantics=("parallel",), vmem_limit_bytes=VMEM_LIMIT),
        name="gmlp_mixer",
    )(h, g.reshape(1, d), w_in.astype(BF16), v_gain.reshape(1, width), w_s, b_s.T,
      w_out.astype(BF16))


def _gelu_times(x, half_up):
    c = math.sqrt(2.0 / math.pi)
    inner = x * ((x * x) * (c * 0.044715) + c)
    return (x * (1.0 + jnp.tanh(inner))) * half_up


def _ffn_kernel(*refs, tiles_per_seq, fc, down_group, final_norm, with_proj):
    if with_proj:
        h_ref, attn_ref, wo_ref = refs[:3]
        refs = refs[3:]
    else:
        h_ref = refs[0]
        refs = refs[1:]
    g_ref, wup_ref, cw_ref, cb_ref, wdown_ref, fg_ref, o_ref, carry_ref = refs
    tm = h_ref.shape[0]
    d_ff = wdown_ref.shape[0]
    n_chunks = d_ff // fc

    @pl.when(pl.program_id(0) % tiles_per_seq == 0)
    def _():
        carry_ref[...] = jnp.zeros_like(carry_ref)

    h = h_ref[...]
    if with_proj:
        h = h + _dot(attn_ref[...], wo_ref[...])
    y = _rms(h, g_ref[...]).astype(BF16)
    sublane = lax.broadcasted_iota(jnp.int32, (SUBLANES, fc), 0)

    def cols_of(j, half):
        return slice(half * d_ff + j * fc, half * d_ff + (j + 1) * fc)

    def up_proj(j):
        return [_dot(y, wup_ref[:, cols_of(j, half)]) for half in range(2)]

    def conv_act(j, a_pair):
        conv = []
        for half in range(2):
            cols = cols_of(j, half)
            a = a_pair[half]
            prev = carry_ref[:, cols]
            carry_ref[:, cols] = a[tm - SUBLANES:, :]
            taps = []
            for shift in (2, 1):
                rolled = pltpu.roll(a, shift, 0)
                head = jnp.where(sublane < shift, pltpu.roll(prev, shift, 0), rolled[:SUBLANES, :])
                taps.append(jnp.concatenate([head, rolled[SUBLANES:, :]], axis=0))
            taps.append(a)
            scale = 1.0 if half == 0 else 0.5
            conv.append(sum((cw_ref[k:k + 1, cols] * scale) * taps[k] for k in range(CONV_WIDTH))
                        + cb_ref[:, cols] * scale)
        return _gelu_times(conv[0], conv[1]).astype(BF16)

    acc = h
    pending = []
    a_next = up_proj(0)
    for j in range(n_chunks):
        a_cur = a_next
        if j + 1 < n_chunks:
            a_next = up_proj(j + 1)
        pending.append(conv_act(j, a_cur))
        if len(pending) == down_group or j + 1 == n_chunks:
            lo = (j + 1 - len(pending)) * fc
            act = pending[0] if len(pending) == 1 else jnp.concatenate(pending, axis=1)
            acc = acc + _dot(act, wdown_ref[lo:(j + 1) * fc, :])
            pending = []
    if final_norm:
        acc = _rms(acc, fg_ref[...])
    o_ref[...] = acc


def _ffn_layer(h, g, w_up, conv_w, conv_b, w_down, final_g, seq, final_norm, attn=None, w_o=None,
               tm=512, fc=256, down_group=5):
    n, d = h.shape
    d_ff = w_down.shape[0]
    with_proj = attn is not None
    kern = functools.partial(_ffn_kernel, tiles_per_seq=seq // tm, fc=fc, down_group=down_group,
                             final_norm=final_norm, with_proj=with_proj)
    row_spec = pl.BlockSpec((tm, d), lambda i: (i, 0))
    proj_specs = [row_spec, _const_spec((d, d))] if with_proj else []
    proj_args = [attn, w_o.astype(BF16)] if with_proj else []
    return pl.pallas_call(
        kern,
        grid=(n // tm,),
        in_specs=[row_spec] + proj_specs + [
            _const_spec((1, d)),
            _const_spec((d, 2 * d_ff)),
            _const_spec((CONV_WIDTH, 2 * d_ff)),
            _const_spec((1, 2 * d_ff)),
            _const_spec((d_ff, d)),
            _const_spec((1, d)),
        ],
        out_specs=row_spec,
        out_shape=jax.ShapeDtypeStruct((n, d), F32),
        scratch_shapes=[pltpu.VMEM((SUBLANES, 2 * d_ff), F32)],
        compiler_params=pltpu.CompilerParams(
            dimension_semantics=("arbitrary",), vmem_limit_bytes=VMEM_LIMIT),
        name="conv_ffn",
    )(h, *proj_args, g.reshape(1, d), w_up.astype(BF16), conv_w, conv_b.reshape(1, 2 * d_ff),
      w_down.astype(BF16), final_g.reshape(1, d))


def _qkv_kernel(h_ref, g_ref, w_ref, cos_ref, sin_ref, qh_ref, ql_ref, k_ref, vt_ref, km_ref):
    d = h_ref.shape[1]
    y = _rms(h_ref[...], g_ref[...]).astype(BF16)
    cos = cos_ref[...]
    sin = sin_ref[...]

    def rope(x):
        heads = []
        for hh in range(N_HEADS):
            xh = x[:, hh * HEAD_DIM:(hh + 1) * HEAD_DIM]
            heads.append(xh * cos + pltpu.roll(xh, HEAD_DIM // 2, 1) * sin)
        return jnp.concatenate(heads, axis=1)

    q = rope(_dot(y, w_ref[:, :d])) * (HEAD_DIM ** -0.5 * math.log2(math.e))
    qh = q.astype(BF16)
    qh_ref[...] = qh
    ql_ref[...] = (q - qh.astype(F32)).astype(BF16)
    k = rope(_dot(y, w_ref[:, d:2 * d]))
    k_ref[...] = k.astype(BF16)
    for r in range(km_ref.shape[0]):
        km_ref[r] = jnp.mean(k[r * MOBA_BLOCK:(r + 1) * MOBA_BLOCK, :], axis=0, keepdims=True)
    vt_ref[...] = _dot(y, w_ref[:, 2 * d:]).T.astype(BF16)


def _qkv_layer(h, g, w_qkv, cos2, sin2, seq, tm=512):
    n, d = h.shape
    assert tm % MOBA_BLOCK == 0 and seq % tm == 0
    tps = seq // tm
    row_spec = pl.BlockSpec((tm, d), lambda i: (i, 0))
    tab_spec = pl.BlockSpec((tm, HEAD_DIM), lambda i: (i % tps, 0))
    return pl.pallas_call(
        _qkv_kernel,
        grid=(n // tm,),
        in_specs=[row_spec, _const_spec((1, d)), _const_spec((d, 3 * d)), tab_spec, tab_spec],
        out_specs=[row_spec, row_spec, row_spec,
                   pl.BlockSpec((d, tm), lambda i: (i // tps, i % tps)),
                   pl.BlockSpec((tm // MOBA_BLOCK, 1, d), lambda i: (i, 0, 0))],
        out_shape=[jax.ShapeDtypeStruct((n, d), BF16)] * 3
        + [jax.ShapeDtypeStruct((n // seq * d, seq), BF16),
           jax.ShapeDtypeStruct((n // MOBA_BLOCK, 1, d), F32)],
        compiler_params=pltpu.CompilerParams(
            dimension_semantics=("parallel",), vmem_limit_bytes=VMEM_LIMIT),
        name="qkv_rope",
    )(h, g.reshape(1, d), w_qkv.astype(BF16), cos2, sin2)


def _attn_kernel(qh_ref, ql_ref, k_ref, vt_ref, km_ref, o_ref):
    seq = qh_ref.shape[0]
    nb = seq // MOBA_BLOCK
    blk = MOBA_BLOCK
    km = km_ref[:, 0, :]
    km_hi = km.astype(BF16).astype(F32)
    km2 = jnp.concatenate([km_hi, km - km_hi], axis=0).astype(BF16)
    blk_id = lax.broadcasted_iota(jnp.int32, (nb, blk), 0)
    key_i = lax.broadcasted_iota(jnp.int32, (blk, blk), 0)
    qry_i = lax.broadcasted_iota(jnp.int32, (blk, blk), 1)
    causal = key_i <= qry_i

    def block(i):
        return slice(i * blk, (i + 1) * blk)

    def scores(i, n):
        return _dot_nt(k_ref[block(n), :], qh_ref[block(i), :])

    def gate_dots(i):
        return _dot_nt(km2, qh_ref[block(i), :]), _dot_nt(km2, ql_ref[block(i), :])

    def softmax_shift(i, s, g):
        s_own = jnp.where(causal, s[i], NEG_INF)
        m = jnp.max(s_own, axis=0, keepdims=True)
        n_sel = min(MOBA_TOPK, i)
        if n_sel == 0:
            return s_own, [-m]
        g_hi, g_lo = g
        gate = g_hi[:nb, :] + (g_hi[nb:, :] + g_lo[:nb, :])
        gate = jnp.where(blk_id < i, gate, NEG_INF)
        rank = jnp.zeros((nb, blk), jnp.int32)
        for mm in range(i):
            gm = gate[mm:mm + 1, :]
            ahead = (gm > gate) | ((gm == gate) & (mm < blk_id))
            rank = rank + ahead.astype(jnp.int32)
        bias = jnp.where((rank < n_sel) & (blk_id < i), 0.0, NEG_INF)
        for n in range(i):
            m = jnp.maximum(m, jnp.max(s[n], axis=0, keepdims=True) + bias[n:n + 1, :])
        return s_own, [bias[n:n + 1, :] - m for n in range(i)] + [-m]

    s_cur, g_cur = [scores(0, 0)], None
    for i in range(nb):
        s_own, shift = softmax_shift(i, s_cur, g_cur)
        s_cur[i] = s_own
        s_next, g_next = [], None
        probs, l_sum = [], None
        for n in range(i + 2):
            if i + 1 < nb:
                s_next.append(scores(i + 1, n))
                if n == 0:
                    g_next = gate_dots(i + 1)
            if n <= i:
                p = jnp.exp2(s_cur[n] + shift[n])
                l_part = jnp.sum(p, axis=0, keepdims=True)
                l_sum = l_part if l_sum is None else l_sum + l_part
                probs.append(p.astype(BF16))
        acc = _dot(vt_ref[:, 0:(i + 1) * blk], jnp.concatenate(probs, axis=0))
        o_ref[block(i), :] = (acc * (1.0 / l_sum)).T.astype(o_ref.dtype)
        s_cur, g_cur = s_next, g_next


def _attn_layer(qh, ql, k, vt, kmean, batch, seq):
    n, d = qh.shape
    nb = seq // MOBA_BLOCK
    head_spec = pl.BlockSpec((seq, HEAD_DIM), lambda b, h: (b, h))
    return pl.pallas_call(
        _attn_kernel,
        grid=(batch, N_HEADS),
        in_specs=[head_spec, head_spec, head_spec,
                  pl.BlockSpec((HEAD_DIM, seq), lambda b, h: (b * N_HEADS + h, 0)),
                  pl.BlockSpec((nb, 1, HEAD_DIM), lambda b, h: (b, 0, h))],
        out_specs=head_spec,
        out_shape=jax.ShapeDtypeStruct((n, d), BF16),
        compiler_params=pltpu.CompilerParams(
            dimension_semantics=("parallel", "parallel"), vmem_limit_bytes=VMEM_LIMIT),
        name="moba_attention",
    )(qh, ql, k, vt, kmean)


def _rope_tables(seq):
    half = HEAD_DIM // 2
    inv = ROPE_THETA ** (-jnp.arange(half, dtype=F32) / half)
    ang = jnp.arange(seq).astype(F32)[:, None] * inv[None, :]
    cos, sin = jnp.cos(ang), jnp.sin(ang)
    return jnp.concatenate([cos, cos], axis=-1), jnp.concatenate([-sin, sin], axis=-1)


def kernel(x, mix_norm, a_w_in, a_v_gain, a_w_s, a_b_s, a_w_out, b_w_qkv, b_w_o, ffn_norm,
           ffn_w_up, ffn_conv_w, ffn_conv_b, ffn_w_down, final_norm):
    batch, seq, d = x.shape
    assert seq % MOBA_BLOCK == 0 and d == N_HEADS * HEAD_DIM
    h = x.reshape(batch * seq, d)

    h = _gmlp_layer(h, mix_norm[0], a_w_in[0], a_v_gain[0], a_w_s[0], a_b_s[0], a_w_out[0])
    h = _ffn_layer(h, ffn_norm[0], ffn_w_up[0], ffn_conv_w[0], ffn_conv_b[0], ffn_w_down[0],
                   final_norm, seq, final_norm=False)

    cos2, sin2 = _rope_tables(seq)
    qh, ql, k, vt, kmean = _qkv_layer(h, mix_norm[1], b_w_qkv[0], cos2, sin2, seq)
    attn = _attn_layer(qh, ql, k, vt, kmean, batch, seq)
    h = _ffn_layer(h, ffn_norm[1], ffn_w_up[1], ffn_conv_w[1], ffn_conv_b[1], ffn_w_down[1],
                   final_norm, seq, final_norm=True, attn=attn, w_o=b_w_o[0])
    return h.reshape(batch, seq, d)
```

```python
import functools
import math

import jax
import jax.numpy as jnp
from jax import lax
from jax.experimental import pallas as pl
from jax.experimental.pallas import tpu as pltpu

F32 = jnp.float32
BF16 = jnp.bfloat16

EPS = 1e-6
NEG_INF = -1e30
GMLP_CHUNK = 128
GMLP_GROUPS = 8
N_HEADS = 8
HEAD_DIM = 128
MOBA_BLOCK = 256
MOBA_TOPK = 3
ROPE_THETA = 10000.0
CONV_WIDTH = 3
SUBLANES = 8
BF16_SUBLANES = 16
VMEM_LIMIT = 56 * 1024 * 1024


def _rms(x, g):
    return x * lax.rsqrt(jnp.mean(x * x, axis=-1, keepdims=True) + EPS) * g


def _gelu(x):
    c = math.sqrt(2.0 / math.pi)
    return x * (0.5 * (1.0 + jnp.tanh(c * (x + 0.044715 * (x * x * x)))))


def _dot(a, b):
    return jnp.dot(a, b, preferred_element_type=F32)


def _dot_nt(a, b):
    return lax.dot_general(a, b, (((1,), (1,)), ((), ())), preferred_element_type=F32)


def _const_spec(shape):
    return pl.BlockSpec(shape, lambda *_: (0,) * len(shape), pipeline_mode=pl.Buffered(1))


def _cast_plan(weights, n_steps):
    in_specs, out_specs, out_shapes = [], [], []
    for w in weights:
        rows, cols = w.shape
        n_slices = max(s for s in range(1, n_steps + 1)
                       if rows % s == 0 and (rows // s) % BF16_SUBLANES == 0)
        spec = pl.BlockSpec((rows // n_slices, cols),
                            lambda i, last=n_slices - 1: (jnp.minimum(i, last), 0))
        in_specs.append(spec)
        out_specs.append(spec)
        out_shapes.append(jax.ShapeDtypeStruct((rows, cols), BF16))
    return in_specs, out_specs, out_shapes


def _cast_slices(src_refs, dst_refs):
    for src, dst in zip(src_refs, dst_refs, strict=True):
        dst[...] = src[...].astype(BF16)


def _gmlp_kernel(*refs, n_cast, out_group):
    h_ref, g_ref, win_ref, vg_ref, ws_ref, bst_ref, wout_ref = refs[:7]
    cast_src = refs[7:7 + n_cast]
    o_ref = refs[7 + n_cast]
    cast_dst = refs[8 + n_cast:8 + 2 * n_cast]
    v_scr = refs[8 + 2 * n_cast]
    _cast_slices(cast_src, cast_dst)
    tm = h_ref.shape[0]
    width = vg_ref.shape[1]
    gdim = width // GMLP_GROUPS
    h = h_ref[...]
    y = _rms(h, g_ref[...]).astype(BF16)
    v = _gelu(_dot(y, win_ref[:, width:]))
    v = _rms(v, vg_ref[...])
    v_scr[...] = v.astype(BF16)
    t_idx = lax.broadcasted_iota(jnp.int32, (GMLP_CHUNK, GMLP_CHUNK), 0)
    s_idx = lax.broadcasted_iota(jnp.int32, (GMLP_CHUNK, GMLP_CHUNK), 1)
    causal = s_idx <= t_idx

    def group_cols(g):
        return slice(g * gdim, (g + 1) * gdim)

    def group_dots(g):
        cols = group_cols(g)
        ws = jnp.where(causal, ws_ref[g], 0.0).astype(BF16)
        mixed = [_dot(ws, v_scr[c * GMLP_CHUNK:(c + 1) * GMLP_CHUNK, cols])
                 for c in range(tm // GMLP_CHUNK)]
        return _dot(y, win_ref[:, cols]), mixed

    acc = h
    nxt = group_dots(0)
    pending = []
    for g in range(GMLP_GROUPS):
        u_pre, mixed = nxt
        if g + 1 < GMLP_GROUPS:
            nxt = group_dots(g + 1)
        bias = bst_ref[:, g:g + 1]
        s = jnp.concatenate([mx + bias for mx in mixed], axis=0)
        pending.append((_gelu(u_pre) * s).astype(BF16))
        if len(pending) == out_group or g + 1 == GMLP_GROUPS:
            lo = (g + 1 - len(pending)) * gdim
            gated = pending[0] if len(pending) == 1 else jnp.concatenate(pending, axis=1)
            acc = acc + _dot(gated, wout_ref[lo:(g + 1) * gdim, :])
            pending = []
    o_ref[...] = acc


def _gmlp_layer(h, g, w_in, v_gain, w_s, b_s, w_out, cast=(), tm=512, out_group=4):
    n, d = h.shape
    width = v_gain.shape[-1]
    cast_in, cast_out, cast_shapes = _cast_plan(cast, n // tm)
    return pl.pallas_call(
        functools.partial(_gmlp_kernel, n_cast=len(cast), out_group=out_group),
        grid=(n // tm,),
        in_specs=[
            pl.BlockSpec((tm, d), lambda i: (i, 0)),
            _const_spec((1, d)),
            _const_spec((d, 2 * width)),
            _const_spec((1, width)),
            _const_spec(w_s.shape),
            _const_spec((GMLP_CHUNK, GMLP_GROUPS)),
            _const_spec((width, d)),
        ] + cast_in,
        out_specs=[pl.BlockSpec((tm, d), lambda i: (i, 0))] + cast_out,
        out_shape=[jax.ShapeDtypeStruct((n, d), F32)] + cast_shapes,
        scratch_shapes=[pltpu.VMEM((tm, width), BF16)],
        compiler_params=pltpu.CompilerParams(
            dimension_semantics=("arbitrary",), vmem_limit_bytes=VMEM_LIMIT),
        name="gmlp_mixer",
    )(h, g.reshape(1, d), w_in.astype(BF16), v_gain.reshape(1, width), w_s, b_s.T,
      w_out.astype(BF16), *cast)


def _gelu_times(x, half_up):
    c = math.sqrt(2.0 / math.pi)
    inner = x * ((x * x) * (c * 0.044715) + c)
    return (x * (1.0 + jnp.tanh(inner))) * half_up


def _ffn_kernel(*refs, tiles_per_seq, fc, down_group, final_norm, with_proj, n_cast):
    if with_proj:
        h_ref, attn_ref, wo_ref = refs[:3]
        refs = refs[3:]
    else:
        h_ref = refs[0]
        refs = refs[1:]
    g_ref, wup_ref, cw_ref, cb_ref, wdown_ref, fg_ref = refs[:6]
    cast_src = refs[6:6 + n_cast]
    o_ref = refs[6 + n_cast]
    cast_dst = refs[7 + n_cast:7 + 2 * n_cast]
    carry_ref = refs[7 + 2 * n_cast]
    _cast_slices(cast_src, cast_dst)
    tm = h_ref.shape[0]
    d_ff = wdown_ref.shape[0]
    n_chunks = d_ff // fc

    @pl.when(pl.program_id(0) % tiles_per_seq == 0)
    def _():
        carry_ref[...] = jnp.zeros_like(carry_ref)

    h = h_ref[...]
    if with_proj:
        h = h + _dot(attn_ref[...], wo_ref[...])
    y = _rms(h, g_ref[...]).astype(BF16)
    sublane = lax.broadcasted_iota(jnp.int32, (SUBLANES, fc), 0)

    def cols_of(j, half):
        return slice(half * d_ff + j * fc, half * d_ff + (j + 1) * fc)

    def up_proj(j):
        return [_dot(y, wup_ref[:, cols_of(j, half)]) for half in range(2)]

    def conv_act(j, a_pair):
        conv = []
        for half in range(2):
            cols = cols_of(j, half)
            a = a_pair[half]
            prev = carry_ref[:, cols]
            carry_ref[:, cols] = a[tm - SUBLANES:, :]
            taps = []
            for shift in (2, 1):
                rolled = pltpu.roll(a, shift, 0)
                head = jnp.where(sublane < shift, pltpu.roll(prev, shift, 0), rolled[:SUBLANES, :])
                taps.append(jnp.concatenate([head, rolled[SUBLANES:, :]], axis=0))
            taps.append(a)
            scale = 1.0 if half == 0 else 0.5
            conv.append(sum((cw_ref[k:k + 1, cols] * scale) * taps[k] for k in range(CONV_WIDTH))
                        + cb_ref[:, cols] * scale)
        return _gelu_times(conv[0], conv[1]).astype(BF16)

    acc = h
    pending = []
    a_next = up_proj(0)
    for j in range(n_chunks):
        a_cur = a_next
        if j + 1 < n_chunks:
            a_next = up_proj(j + 1)
        pending.append(conv_act(j, a_cur))
        if len(pending) == down_group or j + 1 == n_chunks:
            lo = (j + 1 - len(pending)) * fc
            act = pending[0] if len(pending) == 1 else jnp.concatenate(pending, axis=1)
            acc = acc + _dot(act, wdown_ref[lo:(j + 1) * fc, :])
            pending = []
    if final_norm:
        acc = _rms(acc, fg_ref[...])
    o_ref[...] = acc


def _ffn_layer(h, g, w_up, conv_w, conv_b, w_down, final_g, seq, final_norm, attn=None, w_o=None,
               cast=(), tm=512, fc=256, down_group=5):
    n, d = h.shape
    d_ff = w_down.shape[0]
    with_proj = attn is not None
    kern = functools.partial(_ffn_kernel, tiles_per_seq=seq // tm, fc=fc, down_group=down_group,
                             final_norm=final_norm, with_proj=with_proj, n_cast=len(cast))
    row_spec = pl.BlockSpec((tm, d), lambda i: (i, 0))
    proj_specs = [row_spec, _const_spec((d, d))] if with_proj else []
    proj_args = [attn, w_o.astype(BF16)] if with_proj else []
    cast_in, cast_out, cast_shapes = _cast_plan(cast, n // tm)
    return pl.pallas_call(
        kern,
        grid=(n // tm,),
        in_specs=[row_spec] + proj_specs + [
            _const_spec((1, d)),
            _const_spec((d, 2 * d_ff)),
            _const_spec((CONV_WIDTH, 2 * d_ff)),
            _const_spec((1, 2 * d_ff)),
            _const_spec((d_ff, d)),
            _const_spec((1, d)),
        ] + cast_in,
        out_specs=[row_spec] + cast_out,
        out_shape=[jax.ShapeDtypeStruct((n, d), F32)] + cast_shapes,
        scratch_shapes=[pltpu.VMEM((SUBLANES, 2 * d_ff), F32)],
        compiler_params=pltpu.CompilerParams(
            dimension_semantics=("arbitrary",), vmem_limit_bytes=VMEM_LIMIT),
        name="conv_ffn",
    )(h, *proj_args, g.reshape(1, d), w_up.astype(BF16), conv_w, conv_b.reshape(1, 2 * d_ff),
      w_down.astype(BF16), final_g.reshape(1, d), *cast)


def _qkv_kernel(h_ref, g_ref, w_ref, cos_ref, sin_ref, qh_ref, ql_ref, k_ref, vt_ref, km_ref):
    d = h_ref.shape[1]
    y = _rms(h_ref[...], g_ref[...]).astype(BF16)
    cos = cos_ref[...]
    sin = sin_ref[...]

    def rope(x):
        heads = []
        for hh in range(N_HEADS):
            xh = x[:, hh * HEAD_DIM:(hh + 1) * HEAD_DIM]
            heads.append(xh * cos + pltpu.roll(xh, HEAD_DIM // 2, 1) * sin)
        return jnp.concatenate(heads, axis=1)

    q = rope(_dot(y, w_ref[:, :d])) * (HEAD_DIM ** -0.5 * math.log2(math.e))
    qh = q.astype(BF16)
    qh_ref[...] = qh
    ql_ref[...] = (q - qh.astype(F32)).astype(BF16)
    k = rope(_dot(y, w_ref[:, d:2 * d]))
    k_ref[...] = k.astype(BF16)
    for r in range(km_ref.shape[0]):
        km_ref[r] = jnp.mean(k[r * MOBA_BLOCK:(r + 1) * MOBA_BLOCK, :], axis=0, keepdims=True)
    vt_ref[...] = _dot(y, w_ref[:, 2 * d:]).T.astype(BF16)


def _qkv_layer(h, g, w_qkv, cos2, sin2, seq, tm=512):
    n, d = h.shape
    assert tm % MOBA_BLOCK == 0 and seq % tm == 0
    tps = seq // tm
    row_spec = pl.BlockSpec((tm, d), lambda i: (i, 0))
    tab_spec = pl.BlockSpec((tm, HEAD_DIM), lambda i: (i % tps, 0))
    return pl.pallas_call(
        _qkv_kernel,
        grid=(n // tm,),
        in_specs=[row_spec, _const_spec((1, d)), _const_spec((d, 3 * d)), tab_spec, tab_spec],
        out_specs=[row_spec, row_spec, row_spec,
                   pl.BlockSpec((d, tm), lambda i: (i // tps, i % tps)),
                   pl.BlockSpec((tm // MOBA_BLOCK, 1, d), lambda i: (i, 0, 0))],
        out_shape=[jax.ShapeDtypeStruct((n, d), BF16)] * 3
        + [jax.ShapeDtypeStruct((n // seq * d, seq), BF16),
           jax.ShapeDtypeStruct((n // MOBA_BLOCK, 1, d), F32)],
        compiler_params=pltpu.CompilerParams(
            dimension_semantics=("parallel",), vmem_limit_bytes=VMEM_LIMIT),
        name="qkv_rope",
    )(h, g.reshape(1, d), w_qkv.astype(BF16), cos2, sin2)


def _attn_kernel(qh_ref, ql_ref, k_ref, vt_ref, km_ref, o_ref):
    seq = qh_ref.shape[0]
    nb = seq // MOBA_BLOCK
    blk = MOBA_BLOCK
    km = km_ref[:, 0, :]
    km_hi = km.astype(BF16).astype(F32)
    km2 = jnp.concatenate([km_hi, km - km_hi], axis=0).astype(BF16)
    blk_id = lax.broadcasted_iota(jnp.int32, (nb, blk), 0)
    key_i = lax.broadcasted_iota(jnp.int32, (blk, blk), 0)
    qry_i = lax.broadcasted_iota(jnp.int32, (blk, blk), 1)
    causal = key_i <= qry_i

    def block(i):
        return slice(i * blk, (i + 1) * blk)

    def scores(i, n):
        return _dot_nt(k_ref[block(n), :], qh_ref[block(i), :])

    def gate_dots(i):
        return _dot_nt(km2, qh_ref[block(i), :]), _dot_nt(km2, ql_ref[block(i), :])

    def softmax_shift(i, s, g):
        s_own = jnp.where(causal, s[i], NEG_INF)
        m = jnp.max(s_own, axis=0, keepdims=True)
        n_sel = min(MOBA_TOPK, i)
        if n_sel == 0:
            return s_own, [-m]
        g_hi, g_lo = g
        gate = g_hi[:nb, :] + (g_hi[nb:, :] + g_lo[:nb, :])
        gate = jnp.where(blk_id < i, gate, NEG_INF)
        rank = jnp.zeros((nb, blk), jnp.int32)
        for mm in range(i):
            gm = gate[mm:mm + 1, :]
            ahead = (gm > gate) | ((gm == gate) & (mm < blk_id))
            rank = rank + ahead.astype(jnp.int32)
        bias = jnp.where((rank < n_sel) & (blk_id < i), 0.0, NEG_INF)
        for n in range(i):
            m = jnp.maximum(m, jnp.max(s[n], axis=0, keepdims=True) + bias[n:n + 1, :])
        return s_own, [bias[n:n + 1, :] - m for n in range(i)] + [-m]

    s_cur, g_cur = [scores(0, 0)], None
    for i in range(nb):
        s_own, shift = softmax_shift(i, s_cur, g_cur)
        s_cur[i] = s_own
        s_next, g_next = [], None
        probs, l_sum = [], None
        for n in range(i + 2):
            if i + 1 < nb:
                s_next.append(scores(i + 1, n))
                if n == 0:
                    g_next = gate_dots(i + 1)
            if n <= i:
                p = jnp.exp2(s_cur[n] + shift[n])
                l_part = jnp.sum(p, axis=0, keepdims=True)
                l_sum = l_part if l_sum is None else l_sum + l_part
                probs.append(p.astype(BF16))
        acc = _dot(vt_ref[:, 0:(i + 1) * blk], jnp.concatenate(probs, axis=0))
        o_ref[block(i), :] = (acc * (1.0 / l_sum)).T.astype(o_ref.dtype)
        s_cur, g_cur = s_next, g_next


def _attn_layer(qh, ql, k, vt, kmean, batch, seq):
    n, d = qh.shape
    nb = seq // MOBA_BLOCK
    head_spec = pl.BlockSpec((seq, HEAD_DIM), lambda b, h: (b, h))
    return pl.pallas_call(
        _attn_kernel,
        grid=(batch, N_HEADS),
        in_specs=[head_spec, head_spec, head_spec,
                  pl.BlockSpec((HEAD_DIM, seq), lambda b, h: (b * N_HEADS + h, 0)),
                  pl.BlockSpec((nb, 1, HEAD_DIM), lambda b, h: (b, 0, h))],
        out_specs=head_spec,
        out_shape=jax.ShapeDtypeStruct((n, d), BF16),
        compiler_params=pltpu.CompilerParams(
            dimension_semantics=("parallel", "parallel"), vmem_limit_bytes=VMEM_LIMIT),
        name="moba_attention",
    )(qh, ql, k, vt, kmean)


def _rope_tables(seq):
    half = HEAD_DIM // 2
    inv = ROPE_THETA ** (-jnp.arange(half, dtype=F32) / half)
    ang = jnp.arange(seq).astype(F32)[:, None] * inv[None, :]
    cos, sin = jnp.cos(ang), jnp.sin(ang)
    return jnp.concatenate([cos, cos], axis=-1), jnp.concatenate([-sin, sin], axis=-1)


def kernel(x, mix_norm, a_w_in, a_v_gain, a_w_s, a_b_s, a_w_out, b_w_qkv, b_w_o, ffn_norm,
           ffn_w_up, ffn_conv_w, ffn_conv_b, ffn_w_down, final_norm):
    batch, seq, d = x.shape
    assert seq % MOBA_BLOCK == 0 and d == N_HEADS * HEAD_DIM
    h = x.reshape(batch * seq, d)

    h, w_up0, w_down0 = _gmlp_layer(h, mix_norm[0], a_w_in[0], a_v_gain[0], a_w_s[0], a_b_s[0],
                                    a_w_out[0], cast=(ffn_w_up[0], ffn_w_down[0]))
    h, w_qkv, w_o, w_up1, w_down1 = _ffn_layer(
        h, ffn_norm[0], w_up0, ffn_conv_w[0], ffn_conv_b[0], w_down0, final_norm, seq,
        final_norm=False, cast=(b_w_qkv[0], b_w_o[0], ffn_w_up[1], ffn_w_down[1]))

    cos2, sin2 = _rope_tables(seq)
    qh, ql, k, vt, kmean = _qkv_layer(h, mix_norm[1], w_qkv, cos2, sin2, seq)
    attn = _attn_layer(qh, ql, k, vt, kmean, batch, seq)
    (h,) = _ffn_layer(h, ffn_norm[1], w_up1, ffn_conv_w[1], ffn_conv_b[1], w_down1,
                      final_norm, seq, final_norm=True, attn=attn, w_o=w_o)
    return h.reshape(batch, seq, d)
```

```python
import functools
import math

import jax
import jax.numpy as jnp
from jax import lax
from jax.experimental import pallas as pl
from jax.experimental.pallas import tpu as pltpu

F32 = jnp.float32
BF16 = jnp.bfloat16

EPS = 1e-6
NEG_INF = -1e30
GMLP_CHUNK = 128
GMLP_GROUPS = 8
N_HEADS = 8
HEAD_DIM = 128
MOBA_BLOCK = 256
MOBA_TOPK = 3
ROPE_THETA = 10000.0
CONV_WIDTH = 3
SUBLANES = 8
BF16_SUBLANES = 16
VMEM_LIMIT = 56 * 1024 * 1024


def _rms(x, g):
    return x * lax.rsqrt(jnp.mean(x * x, axis=-1, keepdims=True) + EPS) * g


def _gelu(x):
    c = math.sqrt(2.0 / math.pi)
    return x * (0.5 * (1.0 + jnp.tanh(c * (x + 0.044715 * (x * x * x)))))


def _dot(a, b):
    return jnp.dot(a, b, preferred_element_type=F32)


def _dot_nt(a, b):
    return lax.dot_general(a, b, (((1,), (1,)), ((), ())), preferred_element_type=F32)


def _const_spec(shape):
    return pl.BlockSpec(shape, lambda *_: (0,) * len(shape), pipeline_mode=pl.Buffered(1))


def _cast_plan(layer_weights, n_steps):
    in_specs, out_specs, out_shapes = [], [], []
    for stacked, layer in layer_weights:
        _, rows, cols = stacked.shape
        n_slices = max(s for s in range(1, n_steps + 1)
                       if rows % s == 0 and (rows // s) % BF16_SUBLANES == 0)
        last = n_slices - 1
        in_specs.append(pl.BlockSpec((None, rows // n_slices, cols),
                                     lambda i, layer=layer, last=last: (layer, jnp.minimum(i, last), 0)))
        out_specs.append(pl.BlockSpec((rows // n_slices, cols),
                                      lambda i, last=last: (jnp.minimum(i, last), 0)))
        out_shapes.append(jax.ShapeDtypeStruct((rows, cols), BF16))
    return in_specs, out_specs, out_shapes


def _cast_slices(src_refs, dst_refs):
    for src, dst in zip(src_refs, dst_refs, strict=True):
        dst[...] = src[...].astype(BF16)


def _gmlp_kernel(*refs, n_cast, out_group):
    h_ref, g_ref, win_ref, vg_ref, ws_ref, bst_ref, wout_ref = refs[:7]
    cast_src = refs[7:7 + n_cast]
    o_ref = refs[7 + n_cast]
    cast_dst = refs[8 + n_cast:8 + 2 * n_cast]
    v_scr = refs[8 + 2 * n_cast]
    _cast_slices(cast_src, cast_dst)
    tm = h_ref.shape[0]
    width = vg_ref.shape[1]
    gdim = width // GMLP_GROUPS
    h = h_ref[...]
    y = _rms(h, g_ref[...]).astype(BF16)
    v = _gelu(_dot(y, win_ref[:, width:]))
    v = _rms(v, vg_ref[...])
    v_scr[...] = v.astype(BF16)
    t_idx = lax.broadcasted_iota(jnp.int32, (GMLP_CHUNK, GMLP_CHUNK), 0)
    s_idx = lax.broadcasted_iota(jnp.int32, (GMLP_CHUNK, GMLP_CHUNK), 1)
    causal = s_idx <= t_idx

    def group_cols(g):
        return slice(g * gdim, (g + 1) * gdim)

    def group_dots(g):
        cols = group_cols(g)
        ws = jnp.where(causal, ws_ref[g], 0.0).astype(BF16)
        mixed = [_dot(ws, v_scr[c * GMLP_CHUNK:(c + 1) * GMLP_CHUNK, cols])
                 for c in range(tm // GMLP_CHUNK)]
        return _dot(y, win_ref[:, cols]), mixed

    acc = h
    nxt = group_dots(0)
    pending = []
    for g in range(GMLP_GROUPS):
        u_pre, mixed = nxt
        if g + 1 < GMLP_GROUPS:
            nxt = group_dots(g + 1)
        bias = bst_ref[:, g:g + 1]
        s = jnp.concatenate([mx + bias for mx in mixed], axis=0)
        pending.append((_gelu(u_pre) * s).astype(BF16))
        if len(pending) == out_group or g + 1 == GMLP_GROUPS:
            lo = (g + 1 - len(pending)) * gdim
            gated = pending[0] if len(pending) == 1 else jnp.concatenate(pending, axis=1)
            acc = acc + _dot(gated, wout_ref[lo:(g + 1) * gdim, :])
            pending = []
    o_ref[...] = acc


def _gmlp_layer(h, g, w_in, v_gain, w_s, b_s, w_out, cast=(), tm=512, out_group=4):
    n, d = h.shape
    width = v_gain.shape[-1]
    cast_in, cast_out, cast_shapes = _cast_plan(cast, n // tm)
    return pl.pallas_call(
        functools.partial(_gmlp_kernel, n_cast=len(cast), out_group=out_group),
        grid=(n // tm,),
        in_specs=[
            pl.BlockSpec((tm, d), lambda i: (i, 0)),
            _const_spec((1, d)),
            _const_spec((d, 2 * width)),
            _const_spec((1, width)),
            _const_spec(w_s.shape),
            _const_spec((GMLP_CHUNK, GMLP_GROUPS)),
            _const_spec((width, d)),
        ] + cast_in,
        out_specs=[pl.BlockSpec((tm, d), lambda i: (i, 0))] + cast_out,
        out_shape=[jax.ShapeDtypeStruct((n, d), F32)] + cast_shapes,
        scratch_shapes=[pltpu.VMEM((tm, width), BF16)],
        compiler_params=pltpu.CompilerParams(
            dimension_semantics=("arbitrary",), vmem_limit_bytes=VMEM_LIMIT),
        name="gmlp_mixer",
    )(h, g.reshape(1, d), w_in.astype(BF16), v_gain.reshape(1, width), w_s, b_s.T,
      w_out.astype(BF16), *[stacked for stacked, _ in cast])


def _gelu_times(x, half_up):
    c = math.sqrt(2.0 / math.pi)
    inner = x * ((x * x) * (c * 0.044715) + c)
    return (x * (1.0 + jnp.tanh(inner))) * half_up


def _ffn_kernel(*refs, tiles_per_seq, fc, down_group, final_norm, with_proj, n_cast):
    if with_proj:
        h_ref, attn_ref, wo_ref = refs[:3]
        refs = refs[3:]
    else:
        h_ref = refs[0]
        refs = refs[1:]
    g_ref, wup_ref, cw_ref, cb_ref, wdown_ref, fg_ref = refs[:6]
    cast_src = refs[6:6 + n_cast]
    o_ref = refs[6 + n_cast]
    cast_dst = refs[7 + n_cast:7 + 2 * n_cast]
    carry_ref = refs[7 + 2 * n_cast]
    _cast_slices(cast_src, cast_dst)
    tm = h_ref.shape[0]
    d_ff = wdown_ref.shape[0]
    n_chunks = d_ff // fc
    seg = tm // SUBLANES

    def to_interleaved(x):
        return x.reshape(SUBLANES, seg, x.shape[1]).swapaxes(0, 1).reshape(tm, x.shape[1])

    def from_interleaved(x):
        return x.reshape(seg, SUBLANES, x.shape[1]).swapaxes(0, 1).reshape(tm, x.shape[1])

    @pl.when(pl.program_id(0) % tiles_per_seq == 0)
    def _():
        carry_ref[...] = jnp.zeros_like(carry_ref)

    h = h_ref[...]
    if with_proj:
        h = h + _dot(attn_ref[...], wo_ref[...])
    y = to_interleaved(_rms(h, g_ref[...])).astype(BF16)
    first_sublane = lax.broadcasted_iota(jnp.int32, (SUBLANES, fc), 0) == 0

    def cols_of(j, half):
        return slice(half * d_ff + j * fc, half * d_ff + (j + 1) * fc)

    def up_proj(j):
        return [_dot(y, wup_ref[:, cols_of(j, half)]) for half in range(2)]

    def conv_act(j, a_pair):
        conv = []
        for half in range(2):
            cols = cols_of(j, half)
            a = a_pair[half]
            prev = carry_ref[:, cols]
            carry_ref[:, cols] = a[tm - 2 * SUBLANES:, :]
            wrap = [jnp.where(first_sublane,
                              pltpu.roll(prev[r * SUBLANES:(r + 1) * SUBLANES, :], 1, 0),
                              pltpu.roll(a[tm - (2 - r) * SUBLANES:tm - (1 - r) * SUBLANES, :], 1, 0))
                    for r in range(2)]
            taps = [jnp.concatenate([wrap[0], wrap[1], a[:tm - 2 * SUBLANES, :]], axis=0),
                    jnp.concatenate([wrap[1], a[:tm - SUBLANES, :]], axis=0),
                    a]
            scale = 1.0 if half == 0 else 0.5
            conv.append(sum((cw_ref[k:k + 1, cols] * scale) * taps[k] for k in range(CONV_WIDTH))
                        + cb_ref[:, cols] * scale)
        return _gelu_times(conv[0], conv[1]).astype(BF16)

    acc = None
    pending = []
    a_next = up_proj(0)
    for j in range(n_chunks):
        a_cur = a_next
        if j + 1 < n_chunks:
            a_next = up_proj(j + 1)
        pending.append(conv_act(j, a_cur))
        if len(pending) == down_group or j + 1 == n_chunks:
            lo = (j + 1 - len(pending)) * fc
            act = pending[0] if len(pending) == 1 else jnp.concatenate(pending, axis=1)
            down = _dot(act, wdown_ref[lo:(j + 1) * fc, :])
            acc = down if acc is None else acc + down
            pending = []
    out = h + from_interleaved(acc)
    if final_norm:
        out = _rms(out, fg_ref[...])
    o_ref[...] = out


def _ffn_layer(h, g, w_up, conv_w, conv_b, w_down, final_g, seq, final_norm, attn=None, w_o=None,
               cast=(), tm=512, fc=256, down_group=5):
    n, d = h.shape
    d_ff = w_down.shape[0]
    with_proj = attn is not None
    kern = functools.partial(_ffn_kernel, tiles_per_seq=seq // tm, fc=fc, down_group=down_group,
                             final_norm=final_norm, with_proj=with_proj, n_cast=len(cast))
    row_spec = pl.BlockSpec((tm, d), lambda i: (i, 0))
    proj_specs = [row_spec, _const_spec((d, d))] if with_proj else []
    proj_args = [attn, w_o.astype(BF16)] if with_proj else []
    cast_in, cast_out, cast_shapes = _cast_plan(cast, n // tm)
    return pl.pallas_call(
        kern,
        grid=(n // tm,),
        in_specs=[row_spec] + proj_specs + [
            _const_spec((1, d)),
            _const_spec((d, 2 * d_ff)),
            _const_spec((CONV_WIDTH, 2 * d_ff)),
            _const_spec((1, 2 * d_ff)),
            _const_spec((d_ff, d)),
            _const_spec((1, d)),
        ] + cast_in,
        out_specs=[row_spec] + cast_out,
        out_shape=[jax.ShapeDtypeStruct((n, d), F32)] + cast_shapes,
        scratch_shapes=[pltpu.VMEM((2 * SUBLANES, 2 * d_ff), F32)],
        compiler_params=pltpu.CompilerParams(
            dimension_semantics=("arbitrary",), vmem_limit_bytes=VMEM_LIMIT),
        name="conv_ffn",
    )(h, *proj_args, g.reshape(1, d), w_up.astype(BF16), conv_w, conv_b.reshape(1, 2 * d_ff),
      w_down.astype(BF16), final_g.reshape(1, d), *[stacked for stacked, _ in cast])


def _qkv_kernel(h_ref, g_ref, w_ref, cos_ref, sin_ref, qh_ref, ql_ref, k_ref, vt_ref, km_ref):
    d = h_ref.shape[1]
    y = _rms(h_ref[...], g_ref[...]).astype(BF16)
    cos = cos_ref[...]
    sin = sin_ref[...]

    def rope(x):
        heads = []
        for hh in range(N_HEADS):
            xh = x[:, hh * HEAD_DIM:(hh + 1) * HEAD_DIM]
            heads.append(xh * cos + pltpu.roll(xh, HEAD_DIM // 2, 1) * sin)
        return jnp.concatenate(heads, axis=1)

    q = rope(_dot(y, w_ref[:, :d])) * (HEAD_DIM ** -0.5 * math.log2(math.e))
    qh = q.astype(BF16)
    qh_ref[...] = qh
    ql_ref[...] = (q - qh.astype(F32)).astype(BF16)
    k = rope(_dot(y, w_ref[:, d:2 * d]))
    k_ref[...] = k.astype(BF16)
    for r in range(km_ref.shape[0]):
        km_ref[r] = jnp.mean(k[r * MOBA_BLOCK:(r + 1) * MOBA_BLOCK, :], axis=0, keepdims=True)
    vt_ref[...] = _dot(y, w_ref[:, 2 * d:]).T.astype(BF16)


def _qkv_layer(h, g, w_qkv, cos2, sin2, seq, tm=512):
    n, d = h.shape
    assert tm % MOBA_BLOCK == 0 and seq % tm == 0
    tps = seq // tm
    row_spec = pl.BlockSpec((tm, d), lambda i: (i, 0))
    tab_spec = pl.BlockSpec((tm, HEAD_DIM), lambda i: (i % tps, 0))
    return pl.pallas_call(
        _qkv_kernel,
        grid=(n // tm,),
        in_specs=[row_spec, _const_spec((1, d)), _const_spec((d, 3 * d)), tab_spec, tab_spec],
        out_specs=[row_spec, row_spec, row_spec,
                   pl.BlockSpec((d, tm), lambda i: (i // tps, i % tps)),
                   pl.BlockSpec((tm // MOBA_BLOCK, 1, d), lambda i: (i, 0, 0))],
        out_shape=[jax.ShapeDtypeStruct((n, d), BF16)] * 3
        + [jax.ShapeDtypeStruct((n // seq * d, seq), BF16),
           jax.ShapeDtypeStruct((n // MOBA_BLOCK, 1, d), F32)],
        compiler_params=pltpu.CompilerParams(
            dimension_semantics=("parallel",), vmem_limit_bytes=VMEM_LIMIT),
        name="qkv_rope",
    )(h, g.reshape(1, d), w_qkv.astype(BF16), cos2, sin2)


def _attn_kernel(qh_ref, ql_ref, k_ref, vt_ref, km_ref, o_ref):
    seq = qh_ref.shape[0]
    nb = seq // MOBA_BLOCK
    blk = MOBA_BLOCK
    km = km_ref[:, 0, :]
    km_hi = km.astype(BF16).astype(F32)
    km2 = jnp.concatenate([km_hi, km - km_hi], axis=0).astype(BF16)
    blk_id = lax.broadcasted_iota(jnp.int32, (nb, blk), 0)
    key_i = lax.broadcasted_iota(jnp.int32, (blk, blk), 0)
    qry_i = lax.broadcasted_iota(jnp.int32, (blk, blk), 1)
    causal = key_i <= qry_i

    def block(i):
        return slice(i * blk, (i + 1) * blk)

    def scores(i, n):
        return _dot_nt(k_ref[block(n), :], qh_ref[block(i), :])

    def gate_dots(i):
        return _dot_nt(km2, qh_ref[block(i), :]), _dot_nt(km2, ql_ref[block(i), :])

    def softmax_shift(i, s, g):
        s_own = jnp.where(causal, s[i], NEG_INF)
        m = jnp.max(s_own, axis=0, keepdims=True)
        n_sel = min(MOBA_TOPK, i)
        if n_sel == 0:
            return s_own, [-m]
        g_hi, g_lo = g
        gate = g_hi[:nb, :] + (g_hi[nb:, :] + g_lo[:nb, :])
        gate = jnp.where(blk_id < i, gate, NEG_INF)
        rank = jnp.zeros((nb, blk), jnp.int32)
        for mm in range(i):
            gm = gate[mm:mm + 1, :]
            ahead = (gm > gate) | ((gm == gate) & (mm < blk_id))
            rank = rank + ahead.astype(jnp.int32)
        bias = jnp.where((rank < n_sel) & (blk_id < i), 0.0, NEG_INF)
        for n in range(i):
            m = jnp.maximum(m, jnp.max(s[n], axis=0, keepdims=True) + bias[n:n + 1, :])
        return s_own, [bias[n:n + 1, :] - m for n in range(i)] + [-m]

    s_cur, g_cur = [scores(0, 0)], None
    for i in range(nb):
        s_own, shift = softmax_shift(i, s_cur, g_cur)
        s_cur[i] = s_own
        s_next, g_next = [], None
        probs, l_sum = [], None
        for n in range(i + 2):
            if i + 1 < nb:
                s_next.append(scores(i + 1, n))
                if n == 0:
                    g_next = gate_dots(i + 1)
            if n <= i:
                p = jnp.exp2(s_cur[n] + shift[n])
                l_part = jnp.sum(p, axis=0, keepdims=True)
                l_sum = l_part if l_sum is None else l_sum + l_part
                probs.append(p.astype(BF16))
        acc = _dot(vt_ref[:, 0:(i + 1) * blk], jnp.concatenate(probs, axis=0))
        o_ref[block(i), :] = (acc * (1.0 / l_sum)).T.astype(o_ref.dtype)
        s_cur, g_cur = s_next, g_next


def _attn_layer(qh, ql, k, vt, kmean, batch, seq):
    n, d = qh.shape
    nb = seq // MOBA_BLOCK
    head_spec = pl.BlockSpec((seq, HEAD_DIM), lambda b, h: (b, h))
    return pl.pallas_call(
        _attn_kernel,
        grid=(batch, N_HEADS),
        in_specs=[head_spec, head_spec, head_spec,
                  pl.BlockSpec((HEAD_DIM, seq), lambda b, h: (b * N_HEADS + h, 0)),
                  pl.BlockSpec((nb, 1, HEAD_DIM), lambda b, h: (b, 0, h))],
        out_specs=head_spec,
        out_shape=jax.ShapeDtypeStruct((n, d), BF16),
        compiler_params=pltpu.CompilerParams(
            dimension_semantics=("parallel", "parallel"), vmem_limit_bytes=VMEM_LIMIT),
        name="moba_attention",
    )(qh, ql, k, vt, kmean)


def _rope_tables(seq):
    half = HEAD_DIM // 2
    inv = ROPE_THETA ** (-jnp.arange(half, dtype=F32) / half)
    ang = jnp.arange(seq).astype(F32)[:, None] * inv[None, :]
    cos, sin = jnp.cos(ang), jnp.sin(ang)
    return jnp.concatenate([cos, cos], axis=-1), jnp.concatenate([-sin, sin], axis=-1)


def kernel(x, mix_norm, a_w_in, a_v_gain, a_w_s, a_b_s, a_w_out, b_w_qkv, b_w_o, ffn_norm,
           ffn_w_up, ffn_conv_w, ffn_conv_b, ffn_w_down, final_norm):
    batch, seq, d = x.shape
    assert seq % MOBA_BLOCK == 0 and d == N_HEADS * HEAD_DIM
    h = x.reshape(batch * seq, d)

    h, w_up0, w_down0 = _gmlp_layer(h, mix_norm[0], a_w_in[0], a_v_gain[0], a_w_s[0], a_b_s[0],
                                    a_w_out[0], cast=((ffn_w_up, 0), (ffn_w_down, 0)))
    h, w_qkv, w_o, w_up1, w_down1 = _ffn_layer(
        h, ffn_norm[0], w_up0, ffn_conv_w[0], ffn_conv_b[0], w_down0, final_norm, seq,
        final_norm=False, cast=((b_w_qkv, 0), (b_w_o, 0), (ffn_w_up, 1), (ffn_w_down, 1)))

    cos2, sin2 = _rope_tables(seq)
    qh, ql, k, vt, kmean = _qkv_layer(h, mix_norm[1], w_qkv, cos2, sin2, seq)
    attn = _attn_layer(qh, ql, k, vt, kmean, batch, seq)
    (h,) = _ffn_layer(h, ffn_norm[1], w_up1, ffn_conv_w[1], ffn_conv_b[1], w_down1,
                      final_norm, seq, final_norm=True, attn=attn, w_o=w_o)
    return h.reshape(batch, seq, d)
```

```python
import functools
import math

import jax
import jax.numpy as jnp
from jax import lax
from jax.experimental import pallas as pl
from jax.experimental.pallas import tpu as pltpu

F32 = jnp.float32
BF16 = jnp.bfloat16

EPS = 1e-6
NEG_INF = -1e30
GMLP_CHUNK = 128
GMLP_GROUPS = 8
N_HEADS = 8
HEAD_DIM = 128
MOBA_BLOCK = 256
MOBA_TOPK = 3
ROPE_THETA = 10000.0
CONV_WIDTH = 3
SUBLANES = 8
BF16_SUBLANES = 16
VMEM_LIMIT = 56 * 1024 * 1024


def _rms(x, g):
    return x * lax.rsqrt(jnp.mean(x * x, axis=-1, keepdims=True) + EPS) * g


def _gelu(x):
    c = math.sqrt(2.0 / math.pi)
    return x * (0.5 * (1.0 + jnp.tanh(c * (x + 0.044715 * (x * x * x)))))


def _dot(a, b):
    return jnp.dot(a, b, preferred_element_type=F32)


def _dot_nt(a, b):
    return lax.dot_general(a, b, (((1,), (1,)), ((), ())), preferred_element_type=F32)


def _const_spec(shape):
    return pl.BlockSpec(shape, lambda *_: (0,) * len(shape), pipeline_mode=pl.Buffered(1))


def _cast_plan(layer_weights, n_steps):
    in_specs, out_specs, out_shapes = [], [], []
    for stacked, layer in layer_weights:
        _, rows, cols = stacked.shape
        n_slices = max(s for s in range(1, n_steps + 1)
                       if rows % s == 0 and (rows // s) % BF16_SUBLANES == 0)
        last = n_slices - 1
        in_specs.append(pl.BlockSpec((None, rows // n_slices, cols),
                                     lambda i, layer=layer, last=last: (layer, jnp.minimum(i, last), 0)))
        out_specs.append(pl.BlockSpec((rows // n_slices, cols),
                                      lambda i, last=last: (jnp.minimum(i, last), 0)))
        out_shapes.append(jax.ShapeDtypeStruct((rows, cols), BF16))
    return in_specs, out_specs, out_shapes


def _cast_slices(src_refs, dst_refs):
    for src, dst in zip(src_refs, dst_refs, strict=True):
        dst[...] = src[...].astype(BF16)


def _gmlp_kernel(*refs, n_cast, out_group):
    h_ref, g_ref, win_ref, vg_ref, ws_ref, bst_ref, wout_ref = refs[:7]
    cast_src = refs[7:7 + n_cast]
    o_ref = refs[7 + n_cast]
    cast_dst = refs[8 + n_cast:8 + 2 * n_cast]
    v_scr = refs[8 + 2 * n_cast]
    _cast_slices(cast_src, cast_dst)
    tm = h_ref.shape[0]
    width = vg_ref.shape[1]
    gdim = width // GMLP_GROUPS
    h = h_ref[...]
    y = _rms(h, g_ref[...]).astype(BF16)
    v = _gelu(_dot(y, win_ref[:, width:]))
    v = _rms(v, vg_ref[...])
    v_scr[...] = v.astype(BF16)
    t_idx = lax.broadcasted_iota(jnp.int32, (GMLP_CHUNK, GMLP_CHUNK), 0)
    s_idx = lax.broadcasted_iota(jnp.int32, (GMLP_CHUNK, GMLP_CHUNK), 1)
    causal = s_idx <= t_idx

    def group_cols(g):
        return slice(g * gdim, (g + 1) * gdim)

    def group_dots(g):
        cols = group_cols(g)
        ws = jnp.where(causal, ws_ref[g], 0.0).astype(BF16)
        mixed = [_dot(ws, v_scr[c * GMLP_CHUNK:(c + 1) * GMLP_CHUNK, cols])
                 for c in range(tm // GMLP_CHUNK)]
        return _dot(y, win_ref[:, cols]), mixed

    acc = h
    nxt = group_dots(0)
    pending = []
    for g in range(GMLP_GROUPS):
        u_pre, mixed = nxt
        if g + 1 < GMLP_GROUPS:
            nxt = group_dots(g + 1)
        bias = bst_ref[:, g:g + 1]
        s = jnp.concatenate([mx + bias for mx in mixed], axis=0)
        pending.append((_gelu(u_pre) * s).astype(BF16))
        if len(pending) == out_group or g + 1 == GMLP_GROUPS:
            lo = (g + 1 - len(pending)) * gdim
            gated = pending[0] if len(pending) == 1 else jnp.concatenate(pending, axis=1)
            acc = acc + _dot(gated, wout_ref[lo:(g + 1) * gdim, :])
            pending = []
    o_ref[...] = acc


def _gmlp_layer(h, g, w_in, v_gain, w_s, b_s, w_out, cast=(), tm=512, out_group=4):
    n, d = h.shape
    width = v_gain.shape[-1]
    cast_in, cast_out, cast_shapes = _cast_plan(cast, n // tm)
    return pl.pallas_call(
        functools.partial(_gmlp_kernel, n_cast=len(cast), out_group=out_group),
        grid=(n // tm,),
        in_specs=[
            pl.BlockSpec((tm, d), lambda i: (i, 0)),
            _const_spec((1, d)),
            _const_spec((d, 2 * width)),
            _const_spec((1, width)),
            _const_spec(w_s.shape),
            _const_spec((GMLP_CHUNK, GMLP_GROUPS)),
            _const_spec((width, d)),
        ] + cast_in,
        out_specs=[pl.BlockSpec((tm, d), lambda i: (i, 0))] + cast_out,
        out_shape=[jax.ShapeDtypeStruct((n, d), F32)] + cast_shapes,
        scratch_shapes=[pltpu.VMEM((tm, width), BF16)],
        compiler_params=pltpu.CompilerParams(
            dimension_semantics=("arbitrary",), vmem_limit_bytes=VMEM_LIMIT),
        name="gmlp_mixer",
    )(h, g.reshape(1, d), w_in.astype(BF16), v_gain.reshape(1, width), w_s, b_s.T,
      w_out.astype(BF16), *[stacked for stacked, _ in cast])


def _gelu_times(x, half_up):
    c = math.sqrt(2.0 / math.pi)
    inner = x * ((x * x) * (c * 0.044715) + c)
    return (x * (1.0 + jnp.tanh(inner))) * half_up


def _ffn_kernel(*refs, tiles_per_seq, fc, down_group, final_norm, with_proj, n_cast):
    if with_proj:
        h_ref, attn_ref, wo_ref = refs[:3]
        refs = refs[3:]
    else:
        h_ref = refs[0]
        refs = refs[1:]
    g_ref, wup_ref, cw_ref, cb_ref, wdown_ref, fg_ref = refs[:6]
    cast_src = refs[6:6 + n_cast]
    o_ref = refs[6 + n_cast]
    cast_dst = refs[7 + n_cast:7 + 2 * n_cast]
    carry_ref = refs[7 + 2 * n_cast]
    _cast_slices(cast_src, cast_dst)
    tm = h_ref.shape[0]
    d_ff = wdown_ref.shape[0]
    n_chunks = d_ff // fc
    seg = tm // SUBLANES

    def to_interleaved(x):
        return x.reshape(SUBLANES, seg, x.shape[1]).swapaxes(0, 1).reshape(tm, x.shape[1])

    def from_interleaved(x):
        return x.reshape(seg, SUBLANES, x.shape[1]).swapaxes(0, 1).reshape(tm, x.shape[1])

    @pl.when(pl.program_id(0) % tiles_per_seq == 0)
    def _():
        carry_ref[...] = jnp.zeros_like(carry_ref)

    h = h_ref[...]
    if with_proj:
        h = h + _dot(attn_ref[...], wo_ref[...])
    y = to_interleaved(_rms(h, g_ref[...])).astype(BF16)
    first_sublane = lax.broadcasted_iota(jnp.int32, (SUBLANES, fc), 0) == 0

    def cols_of(j, half):
        return slice(half * d_ff + j * fc, half * d_ff + (j + 1) * fc)

    def up_proj(j):
        return [_dot(y, wup_ref[:, cols_of(j, half)]) for half in range(2)]

    def conv_act(j, a_pair):
        conv = []
        for half in range(2):
            cols = cols_of(j, half)
            a = a_pair[half]
            prev = carry_ref[:, cols]
            carry_ref[:, cols] = a[tm - 2 * SUBLANES:, :]
            wrap = [jnp.where(first_sublane,
                              pltpu.roll(prev[r * SUBLANES:(r + 1) * SUBLANES, :], 1, 0),
                              pltpu.roll(a[tm - (2 - r) * SUBLANES:tm - (1 - r) * SUBLANES, :], 1, 0))
                    for r in range(2)]
            taps = [jnp.concatenate([wrap[0], wrap[1], a[:tm - 2 * SUBLANES, :]], axis=0),
                    jnp.concatenate([wrap[1], a[:tm - SUBLANES, :]], axis=0),
                    a]
            scale = 1.0 if half == 0 else 0.5
            conv.append(sum((cw_ref[k:k + 1, cols] * scale) * taps[k] for k in range(CONV_WIDTH))
                        + cb_ref[:, cols] * scale)
        return _gelu_times(conv[0], conv[1]).astype(BF16)

    acc = None
    pending = []
    a_next = up_proj(0)
    for j in range(n_chunks):
        a_cur = a_next
        if j + 1 < n_chunks:
            a_next = up_proj(j + 1)
        pending.append(conv_act(j, a_cur))
        if len(pending) == down_group or j + 1 == n_chunks:
            lo = (j + 1 - len(pending)) * fc
            act = pending[0] if len(pending) == 1 else jnp.concatenate(pending, axis=1)
            down = _dot(act, wdown_ref[lo:(j + 1) * fc, :])
            acc = down if acc is None else acc + down
            pending = []
    out = h + from_interleaved(acc)
    if final_norm:
        out = _rms(out, fg_ref[...])
    o_ref[...] = out


def _ffn_layer(h, g, w_up, conv_w, conv_b, w_down, final_g, seq, final_norm, attn=None, w_o=None,
               cast=(), tm=512, fc=256, down_group=5):
    n, d = h.shape
    d_ff = w_down.shape[0]
    with_proj = attn is not None
    kern = functools.partial(_ffn_kernel, tiles_per_seq=seq // tm, fc=fc, down_group=down_group,
                             final_norm=final_norm, with_proj=with_proj, n_cast=len(cast))
    row_spec = pl.BlockSpec((tm, d), lambda i: (i, 0))
    proj_specs = [row_spec, _const_spec((d, d))] if with_proj else []
    proj_args = [attn, w_o.astype(BF16)] if with_proj else []
    cast_in, cast_out, cast_shapes = _cast_plan(cast, n // tm)
    return pl.pallas_call(
        kern,
        grid=(n // tm,),
        in_specs=[row_spec] + proj_specs + [
            _const_spec((1, d)),
            _const_spec((d, 2 * d_ff)),
            _const_spec((CONV_WIDTH, 2 * d_ff)),
            _const_spec((1, 2 * d_ff)),
            _const_spec((d_ff, d)),
            _const_spec((1, d)),
        ] + cast_in,
        out_specs=[row_spec] + cast_out,
        out_shape=[jax.ShapeDtypeStruct((n, d), F32)] + cast_shapes,
        scratch_shapes=[pltpu.VMEM((2 * SUBLANES, 2 * d_ff), F32)],
        compiler_params=pltpu.CompilerParams(
            dimension_semantics=("arbitrary",), vmem_limit_bytes=VMEM_LIMIT),
        name="conv_ffn",
    )(h, *proj_args, g.reshape(1, d), w_up.astype(BF16), conv_w, conv_b.reshape(1, 2 * d_ff),
      w_down.astype(BF16), final_g.reshape(1, d), *[stacked for stacked, _ in cast])


def _qkv_kernel(h_ref, g_ref, w_ref, cos_ref, sin_ref, qh_ref, ql_ref, k_ref, vt_ref, km_ref):
    d = h_ref.shape[1]
    y = _rms(h_ref[...], g_ref[...]).astype(BF16)
    cos = cos_ref[...]
    sin = sin_ref[...]

    def rope(x):
        heads = []
        for hh in range(N_HEADS):
            xh = x[:, hh * HEAD_DIM:(hh + 1) * HEAD_DIM]
            heads.append(xh * cos + pltpu.roll(xh, HEAD_DIM // 2, 1) * sin)
        return jnp.concatenate(heads, axis=1)

    q = rope(_dot(y, w_ref[:, :d])) * (HEAD_DIM ** -0.5 * math.log2(math.e))
    qh = q.astype(BF16)
    qh_ref[...] = qh
    ql_ref[...] = (q - qh.astype(F32)).astype(BF16)
    k = rope(_dot(y, w_ref[:, d:2 * d]))
    k_ref[...] = k.astype(BF16)
    for r in range(km_ref.shape[0]):
        km_ref[r] = jnp.mean(k[r * MOBA_BLOCK:(r + 1) * MOBA_BLOCK, :], axis=0, keepdims=True)
    vt_ref[...] = _dot(y, w_ref[:, 2 * d:]).T.astype(BF16)


def _qkv_layer(h, g, w_qkv, cos2, sin2, seq, tm=1024):
    n, d = h.shape
    assert tm % MOBA_BLOCK == 0 and seq % tm == 0
    tps = seq // tm
    row_spec = pl.BlockSpec((tm, d), lambda i: (i, 0))
    tab_spec = pl.BlockSpec((tm, HEAD_DIM), lambda i: (i % tps, 0))
    return pl.pallas_call(
        _qkv_kernel,
        grid=(n // tm,),
        in_specs=[row_spec, _const_spec((1, d)), _const_spec((d, 3 * d)), tab_spec, tab_spec],
        out_specs=[row_spec, row_spec, row_spec,
                   pl.BlockSpec((d, tm), lambda i: (i // tps, i % tps)),
                   pl.BlockSpec((tm // MOBA_BLOCK, 1, d), lambda i: (i, 0, 0))],
        out_shape=[jax.ShapeDtypeStruct((n, d), BF16)] * 3
        + [jax.ShapeDtypeStruct((n // seq * d, seq), BF16),
           jax.ShapeDtypeStruct((n // MOBA_BLOCK, 1, d), F32)],
        compiler_params=pltpu.CompilerParams(
            dimension_semantics=("parallel",), vmem_limit_bytes=VMEM_LIMIT),
        name="qkv_rope",
    )(h, g.reshape(1, d), w_qkv.astype(BF16), cos2, sin2)


def _attn_kernel(qh_ref, ql_ref, k_ref, vt_ref, km_ref, o_ref):
    seq = qh_ref.shape[0]
    n_heads = qh_ref.shape[1] // HEAD_DIM
    nb = seq // MOBA_BLOCK
    blk = MOBA_BLOCK
    blk_id = lax.broadcasted_iota(jnp.int32, (nb, blk), 0)
    key_i = lax.broadcasted_iota(jnp.int32, (blk, blk), 0)
    qry_i = lax.broadcasted_iota(jnp.int32, (blk, blk), 1)
    causal = key_i <= qry_i

    def block(i):
        return slice(i * blk, (i + 1) * blk)

    def head(hh):
        return slice(hh * HEAD_DIM, (hh + 1) * HEAD_DIM)

    km2 = []
    for hh in range(n_heads):
        km = km_ref[:, 0, head(hh)]
        km_hi = km.astype(BF16).astype(F32)
        km2.append(jnp.concatenate([km_hi, km - km_hi], axis=0).astype(BF16))

    def scores(hh, i, n):
        return _dot_nt(k_ref[block(n), head(hh)], qh_ref[block(i), head(hh)])

    def gate_dots(hh, i):
        return (_dot_nt(km2[hh], qh_ref[block(i), head(hh)]),
                _dot_nt(km2[hh], ql_ref[block(i), head(hh)]))

    def softmax_shift(i, s, g):
        s_own = jnp.where(causal, s[i], NEG_INF)
        m = jnp.max(s_own, axis=0, keepdims=True)
        n_sel = min(MOBA_TOPK, i)
        if n_sel == 0:
            return s_own, [-m]
        g_hi, g_lo = g
        gate = g_hi[:nb, :] + (g_hi[nb:, :] + g_lo[:nb, :])
        gate = jnp.where(blk_id < i, gate, NEG_INF)
        rank = jnp.zeros((nb, blk), jnp.int32)
        for mm in range(i):
            gm = gate[mm:mm + 1, :]
            ahead = (gm > gate) | ((gm == gate) & (mm < blk_id))
            rank = rank + ahead.astype(jnp.int32)
        bias = jnp.where((rank < n_sel) & (blk_id < i), 0.0, NEG_INF)
        for n in range(i):
            m = jnp.maximum(m, jnp.max(s[n], axis=0, keepdims=True) + bias[n:n + 1, :])
        return s_own, [bias[n:n + 1, :] - m for n in range(i)] + [-m]

    items = [(hh, i) for i in range(nb) for hh in range(n_heads)]
    s_cur, g_cur = [scores(0, 0, 0)], None
    for t, (hh, i) in enumerate(items):
        s_own, shift = softmax_shift(i, s_cur, g_cur)
        s_cur[i] = s_own
        nxt = items[t + 1] if t + 1 < len(items) else None
        n_next = nxt[1] + 1 if nxt else 0
        s_next, g_next = [], None
        probs, l_sum = [], None
        for n in range(max(i + 1, n_next)):
            if n < n_next:
                s_next.append(scores(nxt[0], nxt[1], n))
                if n == 0 and nxt[1] > 0:
                    g_next = gate_dots(*nxt)
            if n <= i:
                p = jnp.exp2(s_cur[n] + shift[n])
                l_part = jnp.sum(p, axis=0, keepdims=True)
                l_sum = l_part if l_sum is None else l_sum + l_part
                probs.append(p.astype(BF16))
        acc = _dot(vt_ref[head(hh), 0:(i + 1) * blk], jnp.concatenate(probs, axis=0))
        o_ref[block(i), head(hh)] = (acc * (1.0 / l_sum)).T.astype(o_ref.dtype)
        s_cur, g_cur = s_next, g_next


def _attn_layer(qh, ql, k, vt, kmean, batch, seq, heads_per_step=2):
    n, d = qh.shape
    nb = seq // MOBA_BLOCK
    width = heads_per_step * HEAD_DIM
    steps = N_HEADS // heads_per_step
    head_spec = pl.BlockSpec((seq, width), lambda b, h: (b, h))
    return pl.pallas_call(
        _attn_kernel,
        grid=(batch, steps),
        in_specs=[head_spec, head_spec, head_spec,
                  pl.BlockSpec((width, seq), lambda b, h: (b * steps + h, 0)),
                  pl.BlockSpec((nb, 1, width), lambda b, h: (b, 0, h))],
        out_specs=head_spec,
        out_shape=jax.ShapeDtypeStruct((n, d), BF16),
        compiler_params=pltpu.CompilerParams(
            dimension_semantics=("parallel", "parallel"), vmem_limit_bytes=VMEM_LIMIT),
        name="moba_attention",
    )(qh, ql, k, vt, kmean)


def _rope_tables(seq):
    half = HEAD_DIM // 2
    inv = ROPE_THETA ** (-jnp.arange(half, dtype=F32) / half)
    ang = jnp.arange(seq).astype(F32)[:, None] * inv[None, :]
    cos, sin = jnp.cos(ang), jnp.sin(ang)
    return jnp.concatenate([cos, cos], axis=-1), jnp.concatenate([-sin, sin], axis=-1)


def kernel(x, mix_norm, a_w_in, a_v_gain, a_w_s, a_b_s, a_w_out, b_w_qkv, b_w_o, ffn_norm,
           ffn_w_up, ffn_conv_w, ffn_conv_b, ffn_w_down, final_norm):
    batch, seq, d = x.shape
    assert seq % MOBA_BLOCK == 0 and d == N_HEADS * HEAD_DIM
    h = x.reshape(batch * seq, d)

    h, w_up0, w_down0 = _gmlp_layer(h, mix_norm[0], a_w_in[0], a_v_gain[0], a_w_s[0], a_b_s[0],
                                    a_w_out[0], cast=((ffn_w_up, 0), (ffn_w_down, 0)))
    h, w_qkv, w_o, w_up1, w_down1 = _ffn_layer(
        h, ffn_norm[0], w_up0, ffn_conv_w[0], ffn_conv_b[0], w_down0, final_norm, seq,
        final_norm=False, cast=((b_w_qkv, 0), (b_w_o, 0), (ffn_w_up, 1), (ffn_w_down, 1)))

    cos2, sin2 = _rope_tables(seq)
    qh, ql, k, vt, kmean = _qkv_layer(h, mix_norm[1], w_qkv, cos2, sin2, seq)
    attn = _attn_layer(qh, ql, k, vt, kmean, batch, seq)
    (h,) = _ffn_layer(h, ffn_norm[1], w_up1, ffn_conv_w[1], ffn_conv_b[1], w_down1,
                      final_norm, seq, final_norm=True, attn=attn, w_o=w_o)
    return h.reshape(batch, seq, d)
```

```python
import functools
import math

import jax
import jax.numpy as jnp
from jax import lax
from jax.experimental import pallas as pl
from jax.experimental.pallas import tpu as pltpu

F32 = jnp.float32
BF16 = jnp.bfloat16

EPS = 1e-6
NEG_INF = -1e30
GMLP_CHUNK = 128
GMLP_GROUPS = 8
N_HEADS = 8
HEAD_DIM = 128
MOBA_BLOCK = 256
MOBA_TOPK = 3
ROPE_THETA = 10000.0
CONV_WIDTH = 3
SUBLANES = 8
BF16_SUBLANES = 16
VMEM_LIMIT = 56 * 1024 * 1024

GMLP_ROWS = 512
GMLP_OUT_GROUP = 4
FFN_ROWS = 512
FFN_CHUNK = 256
FFN_DOWN_GROUP = 5
QKV_ROWS = 1024
ATTN_HEADS_PER_STEP = 2


def _rms(x, g):
    return x * lax.rsqrt(jnp.mean(x * x, axis=-1, keepdims=True) + EPS) * g


def _gelu_times(x, half_factor):
    c = math.sqrt(2.0 / math.pi)
    inner = x * ((x * x) * (c * 0.044715) + c)
    return (x * (1.0 + jnp.tanh(inner))) * half_factor


def _gelu(x):
    return _gelu_times(x, 0.5)


def _dot(a, b):
    return jnp.dot(a, b, preferred_element_type=F32)


def _dot_nt(a, b):
    return lax.dot_general(a, b, (((1,), (1,)), ((), ())), preferred_element_type=F32)


def _const_spec(shape):
    return pl.BlockSpec(shape, lambda *_: (0,) * len(shape), pipeline_mode=pl.Buffered(1))


def _cast_plan(layer_weights, n_steps):
    in_specs, out_specs, out_shapes = [], [], []
    for stacked, layer in layer_weights:
        _, rows, cols = stacked.shape
        n_slices = max(s for s in range(1, n_steps + 1)
                       if rows % s == 0 and (rows // s) % BF16_SUBLANES == 0)
        last = n_slices - 1
        in_specs.append(pl.BlockSpec((None, rows // n_slices, cols),
                                     lambda i, layer=layer, last=last: (layer, jnp.minimum(i, last), 0)))
        out_specs.append(pl.BlockSpec((rows // n_slices, cols),
                                      lambda i, last=last: (jnp.minimum(i, last), 0)))
        out_shapes.append(jax.ShapeDtypeStruct((rows, cols), BF16))
    return in_specs, out_specs, out_shapes


def _cast_slices(src_refs, dst_refs):
    for src, dst in zip(src_refs, dst_refs, strict=True):
        dst[...] = src[...].astype(BF16)


def _gmlp_kernel(*refs, n_cast, out_group):
    h_ref, g_ref, win_ref, vg_ref, ws_ref, bst_ref, wout_ref = refs[:7]
    cast_src = refs[7:7 + n_cast]
    o_ref = refs[7 + n_cast]
    cast_dst = refs[8 + n_cast:8 + 2 * n_cast]
    v_scr = refs[8 + 2 * n_cast]
    _cast_slices(cast_src, cast_dst)
    tm = h_ref.shape[0]
    width = vg_ref.shape[1]
    gdim = width // GMLP_GROUPS
    h = h_ref[...]
    y = _rms(h, g_ref[...]).astype(BF16)
    v = _gelu(_dot(y, win_ref[:, width:]))
    v = _rms(v, vg_ref[...])
    v_scr[...] = v.astype(BF16)
    t_idx = lax.broadcasted_iota(jnp.int32, (GMLP_CHUNK, GMLP_CHUNK), 0)
    s_idx = lax.broadcasted_iota(jnp.int32, (GMLP_CHUNK, GMLP_CHUNK), 1)
    causal = s_idx <= t_idx

    def group_cols(g):
        return slice(g * gdim, (g + 1) * gdim)

    def group_dots(g):
        cols = group_cols(g)
        ws = jnp.where(causal, ws_ref[g] * 0.5, 0.0).astype(BF16)
        mixed = [_dot(ws, v_scr[c * GMLP_CHUNK:(c + 1) * GMLP_CHUNK, cols])
                 for c in range(tm // GMLP_CHUNK)]
        return _dot(y, win_ref[:, cols]), mixed

    acc = h
    nxt = group_dots(0)
    pending = []
    for g in range(GMLP_GROUPS):
        u_pre, mixed = nxt
        if g + 1 < GMLP_GROUPS:
            nxt = group_dots(g + 1)
        half_bias = bst_ref[:, g:g + 1] * 0.5
        half_s = jnp.concatenate([mx + half_bias for mx in mixed], axis=0)
        pending.append(_gelu_times(u_pre, half_s).astype(BF16))
        if len(pending) == out_group or g + 1 == GMLP_GROUPS:
            lo = (g + 1 - len(pending)) * gdim
            gated = pending[0] if len(pending) == 1 else jnp.concatenate(pending, axis=1)
            acc = acc + _dot(gated, wout_ref[lo:(g + 1) * gdim, :])
            pending = []
    o_ref[...] = acc


def _gmlp_layer(h, g, w_in, v_gain, w_s, b_s, w_out, cast=(), tm=GMLP_ROWS,
                out_group=GMLP_OUT_GROUP):
    n, d = h.shape
    width = v_gain.shape[-1]
    assert n % tm == 0 and tm % GMLP_CHUNK == 0 and w_s.shape == (GMLP_GROUPS, GMLP_CHUNK, GMLP_CHUNK)
    cast_in, cast_out, cast_shapes = _cast_plan(cast, n // tm)
    return pl.pallas_call(
        functools.partial(_gmlp_kernel, n_cast=len(cast), out_group=out_group),
        grid=(n // tm,),
        in_specs=[
            pl.BlockSpec((tm, d), lambda i: (i, 0)),
            _const_spec((1, d)),
            _const_spec((d, 2 * width)),
            _const_spec((1, width)),
            _const_spec(w_s.shape),
            _const_spec((GMLP_CHUNK, GMLP_GROUPS)),
            _const_spec((width, d)),
        ] + cast_in,
        out_specs=[pl.BlockSpec((tm, d), lambda i: (i, 0))] + cast_out,
        out_shape=[jax.ShapeDtypeStruct((n, d), F32)] + cast_shapes,
        scratch_shapes=[pltpu.VMEM((tm, width), BF16)],
        compiler_params=pltpu.CompilerParams(
            dimension_semantics=("arbitrary",), vmem_limit_bytes=VMEM_LIMIT),
        name="gmlp_mixer",
    )(h, g.reshape(1, d), w_in.astype(BF16), v_gain.reshape(1, width), w_s, b_s.T,
      w_out.astype(BF16), *[stacked for stacked, _ in cast])


def _ffn_kernel(*refs, tiles_per_seq, fc, down_group, final_norm, with_proj, n_cast):
    if with_proj:
        h_ref, attn_ref, wo_ref = refs[:3]
        refs = refs[3:]
    else:
        h_ref = refs[0]
        refs = refs[1:]
    g_ref, wup_ref, cw_ref, cb_ref, wdown_ref, fg_ref = refs[:6]
    cast_src = refs[6:6 + n_cast]
    o_ref = refs[6 + n_cast]
    cast_dst = refs[7 + n_cast:7 + 2 * n_cast]
    carry_ref = refs[7 + 2 * n_cast]
    _cast_slices(cast_src, cast_dst)
    tm = h_ref.shape[0]
    d_ff = wdown_ref.shape[0]
    n_chunks = d_ff // fc
    seg = tm // SUBLANES

    def to_interleaved(x):
        return x.reshape(SUBLANES, seg, x.shape[1]).swapaxes(0, 1).reshape(tm, x.shape[1])

    def from_interleaved(x):
        return x.reshape(seg, SUBLANES, x.shape[1]).swapaxes(0, 1).reshape(tm, x.shape[1])

    @pl.when(pl.program_id(0) % tiles_per_seq == 0)
    def _():
        carry_ref[...] = jnp.zeros_like(carry_ref)

    h = h_ref[...]
    if with_proj:
        h = h + _dot(attn_ref[...], wo_ref[...])
    y = to_interleaved(_rms(h, g_ref[...])).astype(BF16)
    first_sublane = lax.broadcasted_iota(jnp.int32, (SUBLANES, fc), 0) == 0

    def cols_of(j, half):
        return slice(half * d_ff + j * fc, half * d_ff + (j + 1) * fc)

    def up_proj(j):
        return [_dot(y, wup_ref[:, cols_of(j, half)]) for half in range(2)]

    def conv_act(j, a_pair):
        conv = []
        for half in range(2):
            cols = cols_of(j, half)
            a = a_pair[half]
            prev = carry_ref[:, cols]
            carry_ref[:, cols] = a[tm - 2 * SUBLANES:, :]
            wrap = [jnp.where(first_sublane,
                              pltpu.roll(prev[r * SUBLANES:(r + 1) * SUBLANES, :], 1, 0),
                              pltpu.roll(a[tm - (2 - r) * SUBLANES:tm - (1 - r) * SUBLANES, :], 1, 0))
                    for r in range(2)]
            taps = [jnp.concatenate([wrap[0], wrap[1], a[:tm - 2 * SUBLANES, :]], axis=0),
                    jnp.concatenate([wrap[1], a[:tm - SUBLANES, :]], axis=0),
                    a]
            scale = 1.0 if half == 0 else 0.5
            conv.append(sum((cw_ref[k:k + 1, cols] * scale) * taps[k] for k in range(CONV_WIDTH))
                        + cb_ref[:, cols] * scale)
        return _gelu_times(conv[0], conv[1]).astype(BF16)

    acc = None
    pending = []
    a_next = up_proj(0)
    for j in range(n_chunks):
        a_cur = a_next
        if j + 1 < n_chunks:
            a_next = up_proj(j + 1)
        pending.append(conv_act(j, a_cur))
        if len(pending) == down_group or j + 1 == n_chunks:
            lo = (j + 1 - len(pending)) * fc
            act = pending[0] if len(pending) == 1 else jnp.concatenate(pending, axis=1)
            down = _dot(act, wdown_ref[lo:(j + 1) * fc, :])
            acc = down if acc is None else acc + down
            pending = []
    out = h + from_interleaved(acc)
    if final_norm:
        out = _rms(out, fg_ref[...])
    o_ref[...] = out


def _ffn_layer(h, g, w_up, conv_w, conv_b, w_down, final_g, seq, final_norm, attn=None, w_o=None,
               cast=(), tm=FFN_ROWS, fc=FFN_CHUNK, down_group=FFN_DOWN_GROUP):
    n, d = h.shape
    d_ff = w_down.shape[0]
    assert seq % tm == 0 and n % seq == 0 and d_ff % fc == 0 and conv_w.shape == (CONV_WIDTH, 2 * d_ff)
    with_proj = attn is not None
    kern = functools.partial(_ffn_kernel, tiles_per_seq=seq // tm, fc=fc, down_group=down_group,
                             final_norm=final_norm, with_proj=with_proj, n_cast=len(cast))
    row_spec = pl.BlockSpec((tm, d), lambda i: (i, 0))
    proj_specs = [row_spec, _const_spec((d, d))] if with_proj else []
    proj_args = [attn, w_o.astype(BF16)] if with_proj else []
    cast_in, cast_out, cast_shapes = _cast_plan(cast, n // tm)
    return pl.pallas_call(
        kern,
        grid=(n // tm,),
        in_specs=[row_spec] + proj_specs + [
            _const_spec((1, d)),
            _const_spec((d, 2 * d_ff)),
            _const_spec((CONV_WIDTH, 2 * d_ff)),
            _const_spec((1, 2 * d_ff)),
            _const_spec((d_ff, d)),
            _const_spec((1, d)),
        ] + cast_in,
        out_specs=[row_spec] + cast_out,
        out_shape=[jax.ShapeDtypeStruct((n, d), F32)] + cast_shapes,
        scratch_shapes=[pltpu.VMEM((2 * SUBLANES, 2 * d_ff), F32)],
        compiler_params=pltpu.CompilerParams(
            dimension_semantics=("arbitrary",), vmem_limit_bytes=VMEM_LIMIT),
        name="conv_ffn",
    )(h, *proj_args, g.reshape(1, d), w_up.astype(BF16), conv_w, conv_b.reshape(1, 2 * d_ff),
      w_down.astype(BF16), final_g.reshape(1, d), *[stacked for stacked, _ in cast])


def _qkv_kernel(h_ref, g_ref, w_ref, cos_ref, sin_ref, qh_ref, ql_ref, k_ref, vt_ref, km_ref):
    d = h_ref.shape[1]
    y = _rms(h_ref[...], g_ref[...]).astype(BF16)
    cos = cos_ref[...]
    sin = sin_ref[...]

    def rope(x):
        heads = []
        for hh in range(N_HEADS):
            xh = x[:, hh * HEAD_DIM:(hh + 1) * HEAD_DIM]
            heads.append(xh * cos + pltpu.roll(xh, HEAD_DIM // 2, 1) * sin)
        return jnp.concatenate(heads, axis=1)

    q = rope(_dot(y, w_ref[:, :d])) * (HEAD_DIM ** -0.5 * math.log2(math.e))
    qh = q.astype(BF16)
    qh_ref[...] = qh
    ql_ref[...] = (q - qh.astype(F32)).astype(BF16)
    k = rope(_dot(y, w_ref[:, d:2 * d]))
    k_ref[...] = k.astype(BF16)
    for r in range(km_ref.shape[0]):
        km_ref[r] = jnp.mean(k[r * MOBA_BLOCK:(r + 1) * MOBA_BLOCK, :], axis=0, keepdims=True)
    vt_ref[...] = _dot(y, w_ref[:, 2 * d:]).T.astype(BF16)


def _qkv_layer(h, g, w_qkv, cos2, sin2, seq, tm=QKV_ROWS):
    n, d = h.shape
    assert tm % MOBA_BLOCK == 0 and seq % tm == 0
    tps = seq // tm
    row_spec = pl.BlockSpec((tm, d), lambda i: (i, 0))
    tab_spec = pl.BlockSpec((tm, HEAD_DIM), lambda i: (i % tps, 0))
    return pl.pallas_call(
        _qkv_kernel,
        grid=(n // tm,),
        in_specs=[row_spec, _const_spec((1, d)), _const_spec((d, 3 * d)), tab_spec, tab_spec],
        out_specs=[row_spec, row_spec, row_spec,
                   pl.BlockSpec((d, tm), lambda i: (i // tps, i % tps)),
                   pl.BlockSpec((tm // MOBA_BLOCK, 1, d), lambda i: (i, 0, 0))],
        out_shape=[jax.ShapeDtypeStruct((n, d), BF16)] * 3
        + [jax.ShapeDtypeStruct((n // seq * d, seq), BF16),
           jax.ShapeDtypeStruct((n // MOBA_BLOCK, 1, d), F32)],
        compiler_params=pltpu.CompilerParams(
            dimension_semantics=("parallel",), vmem_limit_bytes=VMEM_LIMIT),
        name="qkv_rope",
    )(h, g.reshape(1, d), w_qkv.astype(BF16), cos2, sin2)


def _attn_kernel(qh_ref, ql_ref, k_ref, vt_ref, km_ref, o_ref):
    seq = qh_ref.shape[0]
    n_heads = qh_ref.shape[1] // HEAD_DIM
    nb = seq // MOBA_BLOCK
    blk = MOBA_BLOCK
    blk_id = lax.broadcasted_iota(jnp.int32, (nb, blk), 0)
    key_i = lax.broadcasted_iota(jnp.int32, (blk, blk), 0)
    qry_i = lax.broadcasted_iota(jnp.int32, (blk, blk), 1)
    causal = key_i <= qry_i

    def block(i):
        return slice(i * blk, (i + 1) * blk)

    def head(hh):
        return slice(hh * HEAD_DIM, (hh + 1) * HEAD_DIM)

    km2 = []
    for hh in range(n_heads):
        km = km_ref[:, 0, head(hh)]
        km_hi = km.astype(BF16).astype(F32)
        km2.append(jnp.concatenate([km_hi, km - km_hi], axis=0).astype(BF16))

    def scores(hh, i, n):
        return _dot_nt(k_ref[block(n), head(hh)], qh_ref[block(i), head(hh)])

    def gate_dots(hh, i):
        return (_dot_nt(km2[hh], qh_ref[block(i), head(hh)]),
                _dot_nt(km2[hh], ql_ref[block(i), head(hh)]))

    def softmax_shift(i, s, g):
        s_own = jnp.where(causal, s[i], NEG_INF)
        m = jnp.max(s_own, axis=0, keepdims=True)
        n_sel = min(MOBA_TOPK, i)
        if n_sel == 0:
            return s_own, [-m]
        g_hi, g_lo = g
        gate = g_hi[:nb, :] + (g_hi[nb:, :] + g_lo[:nb, :])
        gate = jnp.where(blk_id < i, gate, NEG_INF)
        rank = jnp.zeros((nb, blk), jnp.int32)
        for mm in range(i):
            gm = gate[mm:mm + 1, :]
            ahead = (gm > gate) | ((gm == gate) & (mm < blk_id))
            rank = rank + ahead.astype(jnp.int32)
        bias = jnp.where((rank < n_sel) & (blk_id < i), 0.0, NEG_INF)
        for n in range(i):
            m = jnp.maximum(m, jnp.max(s[n], axis=0, keepdims=True) + bias[n:n + 1, :])
        return s_own, [bias[n:n + 1, :] - m for n in range(i)] + [-m]

    items = [(hh, i) for i in range(nb) for hh in range(n_heads)]
    s_cur, g_cur = [scores(0, 0, 0)], None
    for t, (hh, i) in enumerate(items):
        s_own, shift = softmax_shift(i, s_cur, g_cur)
        s_cur[i] = s_own
        nxt = items[t + 1] if t + 1 < len(items) else None
        n_next = nxt[1] + 1 if nxt else 0
        s_next, g_next = [], None
        probs, l_sum = [], None
        for n in range(max(i + 1, n_next)):
            if n < n_next:
                s_next.append(scores(nxt[0], nxt[1], n))
                if n == 0 and nxt[1] > 0:
                    g_next = gate_dots(*nxt)
            if n <= i:
                p = jnp.exp2(s_cur[n] + shift[n])
                l_part = jnp.sum(p, axis=0, keepdims=True)
                l_sum = l_part if l_sum is None else l_sum + l_part
                probs.append(p.astype(BF16))
        acc = _dot(vt_ref[head(hh), 0:(i + 1) * blk], jnp.concatenate(probs, axis=0))
        o_ref[block(i), head(hh)] = (acc * (1.0 / l_sum)).T.astype(o_ref.dtype)
        s_cur, g_cur = s_next, g_next


def _attn_layer(qh, ql, k, vt, kmean, batch, seq, heads_per_step=ATTN_HEADS_PER_STEP):
    n, d = qh.shape
    nb = seq // MOBA_BLOCK
    width = heads_per_step * HEAD_DIM
    steps = N_HEADS // heads_per_step
    head_spec = pl.BlockSpec((seq, width), lambda b, h: (b, h))
    return pl.pallas_call(
        _attn_kernel,
        grid=(batch, steps),
        in_specs=[head_spec, head_spec, head_spec,
                  pl.BlockSpec((width, seq), lambda b, h: (b * steps + h, 0)),
                  pl.BlockSpec((nb, 1, width), lambda b, h: (b, 0, h))],
        out_specs=head_spec,
        out_shape=jax.ShapeDtypeStruct((n, d), BF16),
        compiler_params=pltpu.CompilerParams(
            dimension_semantics=("parallel", "parallel"), vmem_limit_bytes=VMEM_LIMIT),
        name="moba_attention",
    )(qh, ql, k, vt, kmean)


def _rope_tables(seq):
    half = HEAD_DIM // 2
    inv = ROPE_THETA ** (-jnp.arange(half, dtype=F32) / half)
    ang = jnp.arange(seq).astype(F32)[:, None] * inv[None, :]
    cos, sin = jnp.cos(ang), jnp.sin(ang)
    return jnp.concatenate([cos, cos], axis=-1), jnp.concatenate([-sin, sin], axis=-1)


def kernel(x, mix_norm, a_w_in, a_v_gain, a_w_s, a_b_s, a_w_out, b_w_qkv, b_w_o, ffn_norm,
           ffn_w_up, ffn_conv_w, ffn_conv_b, ffn_w_down, final_norm):
    batch, seq, d = x.shape
    assert seq % MOBA_BLOCK == 0 and d == N_HEADS * HEAD_DIM
    h = x.reshape(batch * seq, d)

    h, w_up0, w_down0 = _gmlp_layer(h, mix_norm[0], a_w_in[0], a_v_gain[0], a_w_s[0], a_b_s[0],
                                    a_w_out[0], cast=((ffn_w_up, 0), (ffn_w_down, 0)))
    h, w_qkv, w_o, w_up1, w_down1 = _ffn_layer(
        h, ffn_norm[0], w_up0, ffn_conv_w[0], ffn_conv_b[0], w_down0, final_norm, seq,
        final_norm=False, cast=((b_w_qkv, 0), (b_w_o, 0), (ffn_w_up, 1), (ffn_w_down, 1)))

    cos2, sin2 = _rope_tables(seq)
    qh, ql, k, vt, kmean = _qkv_layer(h, mix_norm[1], w_qkv, cos2, sin2, seq)
    attn = _attn_layer(qh, ql, k, vt, kmean, batch, seq)
    (h,) = _ffn_layer(h, ffn_norm[1], w_up1, ffn_conv_w[1], ffn_conv_b[1], w_down1,
                      final_norm, seq, final_norm=True, attn=attn, w_o=w_o)
    return h.reshape(batch, seq, d)
```

```python
import functools
import math

import jax
import jax.numpy as jnp
from jax import lax
from jax.experimental import pallas as pl
from jax.experimental.pallas import tpu as pltpu

F32 = jnp.float32
BF16 = jnp.bfloat16

EPS = 1e-6
NEG_INF = -1e30
GMLP_CHUNK = 128
GMLP_GROUPS = 8
N_HEADS = 8
HEAD_DIM = 128
MOBA_BLOCK = 256
MOBA_TOPK = 3
ROPE_THETA = 10000.0
CONV_WIDTH = 3
SUBLANES = 8
BF16_SUBLANES = 16
VMEM_LIMIT = 56 * 1024 * 1024

GMLP_ROWS = 512
GMLP_OUT_GROUP = 4
FFN_ROWS = 512
FFN_CHUNK = 256
FFN_DOWN_GROUP = 5
QKV_ROWS = 1024
ATTN_HEADS_PER_STEP = 4


def _rms(x, g):
    return x * lax.rsqrt(jnp.mean(x * x, axis=-1, keepdims=True) + EPS) * g


def _gelu_times(x, half_factor):
    c = math.sqrt(2.0 / math.pi)
    inner = x * ((x * x) * (c * 0.044715) + c)
    return (x * (1.0 + jnp.tanh(inner))) * half_factor


def _gelu(x):
    return _gelu_times(x, 0.5)


def _dot(a, b):
    return jnp.dot(a, b, preferred_element_type=F32)


def _dot_nt(a, b):
    return lax.dot_general(a, b, (((1,), (1,)), ((), ())), preferred_element_type=F32)


def _const_spec(shape):
    return pl.BlockSpec(shape, lambda *_: (0,) * len(shape), pipeline_mode=pl.Buffered(1))


def _cast_plan(layer_weights, n_steps):
    in_specs, out_specs, out_shapes = [], [], []
    for stacked, layer in layer_weights:
        _, rows, cols = stacked.shape
        n_slices = max(s for s in range(1, n_steps + 1)
                       if rows % s == 0 and (rows // s) % BF16_SUBLANES == 0)
        last = n_slices - 1
        in_specs.append(pl.BlockSpec((None, rows // n_slices, cols),
                                     lambda i, layer=layer, last=last: (layer, jnp.minimum(i, last), 0)))
        out_specs.append(pl.BlockSpec((rows // n_slices, cols),
                                      lambda i, last=last: (jnp.minimum(i, last), 0)))
        out_shapes.append(jax.ShapeDtypeStruct((rows, cols), BF16))
    return in_specs, out_specs, out_shapes


def _cast_slices(src_refs, dst_refs):
    for src, dst in zip(src_refs, dst_refs, strict=True):
        dst[...] = src[...].astype(BF16)


def _gmlp_kernel(*refs, n_cast, out_group):
    h_ref, g_ref, win_ref, vg_ref, ws_ref, bst_ref, wout_ref = refs[:7]
    cast_src = refs[7:7 + n_cast]
    o_ref = refs[7 + n_cast]
    cast_dst = refs[8 + n_cast:8 + 2 * n_cast]
    v_scr = refs[8 + 2 * n_cast]
    _cast_slices(cast_src, cast_dst)
    tm = h_ref.shape[0]
    width = vg_ref.shape[1]
    gdim = width // GMLP_GROUPS
    h = h_ref[...]
    y = _rms(h, g_ref[...]).astype(BF16)
    v = _gelu(_dot(y, win_ref[:, width:]))
    v = _rms(v, vg_ref[...])
    v_scr[...] = v.astype(BF16)
    t_idx = lax.broadcasted_iota(jnp.int32, (GMLP_CHUNK, GMLP_CHUNK), 0)
    s_idx = lax.broadcasted_iota(jnp.int32, (GMLP_CHUNK, GMLP_CHUNK), 1)
    causal = s_idx <= t_idx

    def group_cols(g):
        return slice(g * gdim, (g + 1) * gdim)

    def group_dots(g):
        cols = group_cols(g)
        ws = jnp.where(causal, ws_ref[g] * 0.5, 0.0).astype(BF16)
        mixed = [_dot(ws, v_scr[c * GMLP_CHUNK:(c + 1) * GMLP_CHUNK, cols])
                 for c in range(tm // GMLP_CHUNK)]
        return _dot(y, win_ref[:, cols]), mixed

    acc = h
    nxt = group_dots(0)
    pending = []
    for g in range(GMLP_GROUPS):
        u_pre, mixed = nxt
        if g + 1 < GMLP_GROUPS:
            nxt = group_dots(g + 1)
        half_bias = bst_ref[:, g:g + 1] * 0.5
        half_s = jnp.concatenate([mx + half_bias for mx in mixed], axis=0)
        pending.append(_gelu_times(u_pre, half_s).astype(BF16))
        if len(pending) == out_group or g + 1 == GMLP_GROUPS:
            lo = (g + 1 - len(pending)) * gdim
            gated = pending[0] if len(pending) == 1 else jnp.concatenate(pending, axis=1)
            acc = acc + _dot(gated, wout_ref[lo:(g + 1) * gdim, :])
            pending = []
    o_ref[...] = acc


def _gmlp_layer(h, g, w_in, v_gain, w_s, b_s, w_out, cast=(), tm=GMLP_ROWS,
                out_group=GMLP_OUT_GROUP):
    n, d = h.shape
    width = v_gain.shape[-1]
    assert n % tm == 0 and tm % GMLP_CHUNK == 0 and w_s.shape == (GMLP_GROUPS, GMLP_CHUNK, GMLP_CHUNK)
    cast_in, cast_out, cast_shapes = _cast_plan(cast, n // tm)
    return pl.pallas_call(
        functools.partial(_gmlp_kernel, n_cast=len(cast), out_group=out_group),
        grid=(n // tm,),
        in_specs=[
            pl.BlockSpec((tm, d), lambda i: (i, 0)),
            _const_spec((1, d)),
            _const_spec((d, 2 * width)),
            _const_spec((1, width)),
            _const_spec(w_s.shape),
            _const_spec((GMLP_CHUNK, GMLP_GROUPS)),
            _const_spec((width, d)),
        ] + cast_in,
        out_specs=[pl.BlockSpec((tm, d), lambda i: (i, 0))] + cast_out,
        out_shape=[jax.ShapeDtypeStruct((n, d), F32)] + cast_shapes,
        scratch_shapes=[pltpu.VMEM((tm, width), BF16)],
        compiler_params=pltpu.CompilerParams(
            dimension_semantics=("arbitrary",), vmem_limit_bytes=VMEM_LIMIT),
        name="gmlp_mixer",
    )(h, g.reshape(1, d), w_in.astype(BF16), v_gain.reshape(1, width), w_s, b_s.T,
      w_out.astype(BF16), *[stacked for stacked, _ in cast])


def _ffn_kernel(*refs, tiles_per_seq, fc, down_group, final_norm, with_proj, n_cast):
    if with_proj:
        h_ref, attn_ref, wo_ref = refs[:3]
        refs = refs[3:]
    else:
        h_ref = refs[0]
        refs = refs[1:]
    g_ref, wup_ref, cw_ref, cb_ref, wdown_ref, fg_ref = refs[:6]
    cast_src = refs[6:6 + n_cast]
    o_ref = refs[6 + n_cast]
    cast_dst = refs[7 + n_cast:7 + 2 * n_cast]
    carry_ref = refs[7 + 2 * n_cast]
    _cast_slices(cast_src, cast_dst)
    tm = h_ref.shape[0]
    d_ff = wdown_ref.shape[0]
    n_chunks = d_ff // fc
    seg = tm // SUBLANES

    def to_interleaved(x):
        return x.reshape(SUBLANES, seg, x.shape[1]).swapaxes(0, 1).reshape(tm, x.shape[1])

    def from_interleaved(x):
        return x.reshape(seg, SUBLANES, x.shape[1]).swapaxes(0, 1).reshape(tm, x.shape[1])

    @pl.when(pl.program_id(0) % tiles_per_seq == 0)
    def _():
        carry_ref[...] = jnp.zeros_like(carry_ref)

    h = h_ref[...]
    if with_proj:
        h = h + _dot(attn_ref[...], wo_ref[...])
    y = to_interleaved(_rms(h, g_ref[...])).astype(BF16)
    first_sublane = lax.broadcasted_iota(jnp.int32, (SUBLANES, fc), 0) == 0

    def cols_of(j, half):
        return slice(half * d_ff + j * fc, half * d_ff + (j + 1) * fc)

    def up_proj(j):
        return [_dot(y, wup_ref[:, cols_of(j, half)]) for half in range(2)]

    def conv_act(j, a_pair):
        conv = []
        for half in range(2):
            cols = cols_of(j, half)
            a = a_pair[half]
            prev = carry_ref[:, cols]
            carry_ref[:, cols] = a[tm - 2 * SUBLANES:, :]
            wrap = [jnp.where(first_sublane,
                              pltpu.roll(prev[r * SUBLANES:(r + 1) * SUBLANES, :], 1, 0),
                              pltpu.roll(a[tm - (2 - r) * SUBLANES:tm - (1 - r) * SUBLANES, :], 1, 0))
                    for r in range(2)]
            taps = [jnp.concatenate([wrap[0], wrap[1], a[:tm - 2 * SUBLANES, :]], axis=0),
                    jnp.concatenate([wrap[1], a[:tm - SUBLANES, :]], axis=0),
                    a]
            scale = 1.0 if half == 0 else 0.5
            conv.append(sum((cw_ref[k:k + 1, cols] * scale) * taps[k] for k in range(CONV_WIDTH))
                        + cb_ref[:, cols] * scale)
        return _gelu_times(conv[0], conv[1]).astype(BF16)

    acc = None
    pending = []
    a_next = up_proj(0)
    for j in range(n_chunks):
        a_cur = a_next
        if j + 1 < n_chunks:
            a_next = up_proj(j + 1)
        pending.append(conv_act(j, a_cur))
        if len(pending) == down_group or j + 1 == n_chunks:
            lo = (j + 1 - len(pending)) * fc
            act = pending[0] if len(pending) == 1 else jnp.concatenate(pending, axis=1)
            down = _dot(act, wdown_ref[lo:(j + 1) * fc, :])
            acc = down if acc is None else acc + down
            pending = []
    out = h + from_interleaved(acc)
    if final_norm:
        out = _rms(out, fg_ref[...])
    o_ref[...] = out


def _ffn_layer(h, g, w_up, conv_w, conv_b, w_down, final_g, seq, final_norm, attn=None, w_o=None,
               cast=(), tm=FFN_ROWS, fc=FFN_CHUNK, down_group=FFN_DOWN_GROUP):
    n, d = h.shape
    d_ff = w_down.shape[0]
    assert seq % tm == 0 and n % seq == 0 and d_ff % fc == 0 and conv_w.shape == (CONV_WIDTH, 2 * d_ff)
    with_proj = attn is not None
    kern = functools.partial(_ffn_kernel, tiles_per_seq=seq // tm, fc=fc, down_group=down_group,
                             final_norm=final_norm, with_proj=with_proj, n_cast=len(cast))
    row_spec = pl.BlockSpec((tm, d), lambda i: (i, 0))
    proj_specs = [row_spec, _const_spec((d, d))] if with_proj else []
    proj_args = [attn, w_o.astype(BF16)] if with_proj else []
    cast_in, cast_out, cast_shapes = _cast_plan(cast, n // tm)
    return pl.pallas_call(
        kern,
        grid=(n // tm,),
        in_specs=[row_spec] + proj_specs + [
            _const_spec((1, d)),
            _const_spec((d, 2 * d_ff)),
            _const_spec((CONV_WIDTH, 2 * d_ff)),
            _const_spec((1, 2 * d_ff)),
            _const_spec((d_ff, d)),
            _const_spec((1, d)),
        ] + cast_in,
        out_specs=[row_spec] + cast_out,
        out_shape=[jax.ShapeDtypeStruct((n, d), F32)] + cast_shapes,
        scratch_shapes=[pltpu.VMEM((2 * SUBLANES, 2 * d_ff), F32)],
        compiler_params=pltpu.CompilerParams(
            dimension_semantics=("arbitrary",), vmem_limit_bytes=VMEM_LIMIT),
        name="conv_ffn",
    )(h, *proj_args, g.reshape(1, d), w_up.astype(BF16), conv_w, conv_b.reshape(1, 2 * d_ff),
      w_down.astype(BF16), final_g.reshape(1, d), *[stacked for stacked, _ in cast])


def _qkv_kernel(h_ref, g_ref, w_ref, cos_ref, sin_ref, qh_ref, ql_ref, k_ref, vt_ref, km_ref):
    d = h_ref.shape[1]
    y = _rms(h_ref[...], g_ref[...]).astype(BF16)
    cos = cos_ref[...]
    sin = sin_ref[...]

    def rope(x):
        heads = []
        for hh in range(N_HEADS):
            xh = x[:, hh * HEAD_DIM:(hh + 1) * HEAD_DIM]
            heads.append(xh * cos + pltpu.roll(xh, HEAD_DIM // 2, 1) * sin)
        return jnp.concatenate(heads, axis=1)

    q = rope(_dot(y, w_ref[:, :d])) * (HEAD_DIM ** -0.5 * math.log2(math.e))
    qh = q.astype(BF16)
    qh_ref[...] = qh
    ql_ref[...] = (q - qh.astype(F32)).astype(BF16)
    k = rope(_dot(y, w_ref[:, d:2 * d]))
    k_ref[...] = k.astype(BF16)
    for r in range(km_ref.shape[0]):
        km_ref[r] = jnp.mean(k[r * MOBA_BLOCK:(r + 1) * MOBA_BLOCK, :], axis=0, keepdims=True)
    vt_ref[...] = _dot(y, w_ref[:, 2 * d:]).T.astype(BF16)


def _qkv_layer(h, g, w_qkv, cos2, sin2, seq, tm=QKV_ROWS):
    n, d = h.shape
    assert tm % MOBA_BLOCK == 0 and seq % tm == 0
    tps = seq // tm
    row_spec = pl.BlockSpec((tm, d), lambda i: (i, 0))
    tab_spec = pl.BlockSpec((tm, HEAD_DIM), lambda i: (i % tps, 0))
    return pl.pallas_call(
        _qkv_kernel,
        grid=(n // tm,),
        in_specs=[row_spec, _const_spec((1, d)), _const_spec((d, 3 * d)), tab_spec, tab_spec],
        out_specs=[row_spec, row_spec, row_spec,
                   pl.BlockSpec((d, tm), lambda i: (i // tps, i % tps)),
                   pl.BlockSpec((tm // MOBA_BLOCK, 1, d), lambda i: (i, 0, 0))],
        out_shape=[jax.ShapeDtypeStruct((n, d), BF16)] * 3
        + [jax.ShapeDtypeStruct((n // seq * d, seq), BF16),
           jax.ShapeDtypeStruct((n // MOBA_BLOCK, 1, d), F32)],
        compiler_params=pltpu.CompilerParams(
            dimension_semantics=("parallel",), vmem_limit_bytes=VMEM_LIMIT),
        name="qkv_rope",
    )(h, g.reshape(1, d), w_qkv.astype(BF16), cos2, sin2)


def _attn_kernel(qh_ref, ql_ref, k_ref, vt_ref, km_ref, o_ref):
    seq = qh_ref.shape[0]
    n_heads = qh_ref.shape[1] // HEAD_DIM
    nb = seq // MOBA_BLOCK
    blk = MOBA_BLOCK
    blk_id = lax.broadcasted_iota(jnp.int32, (nb, blk), 0)
    key_i = lax.broadcasted_iota(jnp.int32, (blk, blk), 0)
    qry_i = lax.broadcasted_iota(jnp.int32, (blk, blk), 1)
    causal = key_i <= qry_i

    def block(i):
        return slice(i * blk, (i + 1) * blk)

    def head(hh):
        return slice(hh * HEAD_DIM, (hh + 1) * HEAD_DIM)

    km2 = []
    for hh in range(n_heads):
        km = km_ref[:, 0, head(hh)]
        km_hi = km.astype(BF16).astype(F32)
        km2.append(jnp.concatenate([km_hi, km - km_hi], axis=0).astype(BF16))

    def scores(hh, i, n):
        return _dot_nt(k_ref[block(n), head(hh)], qh_ref[block(i), head(hh)])

    def gate_dots(hh, i):
        return (_dot_nt(km2[hh], qh_ref[block(i), head(hh)]),
                _dot_nt(km2[hh], ql_ref[block(i), head(hh)]))

    def softmax_shift(i, s, g):
        s_own = jnp.where(causal, s[i], NEG_INF)
        m = jnp.max(s_own, axis=0, keepdims=True)
        n_sel = min(MOBA_TOPK, i)
        if n_sel == 0:
            return s_own, [-m]
        g_hi, g_lo = g
        gate = g_hi[:nb, :] + (g_hi[nb:, :] + g_lo[:nb, :])
        gate = jnp.where(blk_id < i, gate, NEG_INF)
        rank = jnp.zeros((nb, blk), jnp.int32)
        for mm in range(i):
            gm = gate[mm:mm + 1, :]
            ahead = (gm > gate) | ((gm == gate) & (mm < blk_id))
            rank = rank + ahead.astype(jnp.int32)
        bias = jnp.where((rank < n_sel) & (blk_id < i), 0.0, NEG_INF)
        for n in range(i):
            m = jnp.maximum(m, jnp.max(s[n], axis=0, keepdims=True) + bias[n:n + 1, :])
        return s_own, [bias[n:n + 1, :] - m for n in range(i)] + [-m]

    items = [(hh, i) for i in range(nb) for hh in range(n_heads)]
    s_cur, g_cur = [scores(0, 0, 0)], None
    for t, (hh, i) in enumerate(items):
        s_own, shift = softmax_shift(i, s_cur, g_cur)
        s_cur[i] = s_own
        nxt = items[t + 1] if t + 1 < len(items) else None
        n_next = nxt[1] + 1 if nxt else 0
        s_next, g_next = [], None
        probs, l_sum = [], None
        for n in range(max(i + 1, n_next)):
            if n < n_next:
                s_next.append(scores(nxt[0], nxt[1], n))
                if n == 0 and nxt[1] > 0:
                    g_next = gate_dots(*nxt)
            if n <= i:
                p = jnp.exp2(s_cur[n] + shift[n])
                l_part = jnp.sum(p, axis=0, keepdims=True)
                l_sum = l_part if l_sum is None else l_sum + l_part
                probs.append(p.astype(BF16))
        acc = _dot(vt_ref[head(hh), 0:(i + 1) * blk], jnp.concatenate(probs, axis=0))
        o_ref[block(i), head(hh)] = (acc * (1.0 / l_sum)).T.astype(o_ref.dtype)
        s_cur, g_cur = s_next, g_next


def _attn_layer(qh, ql, k, vt, kmean, batch, seq, heads_per_step=ATTN_HEADS_PER_STEP):
    n, d = qh.shape
    nb = seq // MOBA_BLOCK
    width = heads_per_step * HEAD_DIM
    steps = N_HEADS // heads_per_step
    head_spec = pl.BlockSpec((seq, width), lambda b, h: (b, h))
    return pl.pallas_call(
        _attn_kernel,
        grid=(batch, steps),
        in_specs=[head_spec, head_spec, head_spec,
                  pl.BlockSpec((width, seq), lambda b, h: (b * steps + h, 0)),
                  pl.BlockSpec((nb, 1, width), lambda b, h: (b, 0, h))],
        out_specs=head_spec,
        out_shape=jax.ShapeDtypeStruct((n, d), BF16),
        compiler_params=pltpu.CompilerParams(
            dimension_semantics=("parallel", "parallel"), vmem_limit_bytes=VMEM_LIMIT),
        name="moba_attention",
    )(qh, ql, k, vt, kmean)


def _rope_tables(seq):
    half = HEAD_DIM // 2
    inv = ROPE_THETA ** (-jnp.arange(half, dtype=F32) / half)
    ang = jnp.arange(seq).astype(F32)[:, None] * inv[None, :]
    cos, sin = jnp.cos(ang), jnp.sin(ang)
    return jnp.concatenate([cos, cos], axis=-1), jnp.concatenate([-sin, sin], axis=-1)


def kernel(x, mix_norm, a_w_in, a_v_gain, a_w_s, a_b_s, a_w_out, b_w_qkv, b_w_o, ffn_norm,
           ffn_w_up, ffn_conv_w, ffn_conv_b, ffn_w_down, final_norm):
    batch, seq, d = x.shape
    assert seq % MOBA_BLOCK == 0 and d == N_HEADS * HEAD_DIM
    h = x.reshape(batch * seq, d)

    h, w_up0, w_down0 = _gmlp_layer(h, mix_norm[0], a_w_in[0], a_v_gain[0], a_w_s[0], a_b_s[0],
                                    a_w_out[0], cast=((ffn_w_up, 0), (ffn_w_down, 0)))
    h, w_qkv, w_o, w_up1, w_down1 = _ffn_layer(
        h, ffn_norm[0], w_up0, ffn_conv_w[0], ffn_conv_b[0], w_down0, final_norm, seq,
        final_norm=False, cast=((b_w_qkv, 0), (b_w_o, 0), (ffn_w_up, 1), (ffn_w_down, 1)))

    cos2, sin2 = _rope_tables(seq)
    qh, ql, k, vt, kmean = _qkv_layer(h, mix_norm[1], w_qkv, cos2, sin2, seq)
    attn = _attn_layer(qh, ql, k, vt, kmean, batch, seq)
    (h,) = _ffn_layer(h, ffn_norm[1], w_up1, ffn_conv_w[1], ffn_conv_b[1], w_down1,
                      final_norm, seq, final_norm=True, attn=attn, w_o=w_o)
    return h.reshape(batch, seq, d)
```

```python
import functools
import math

import jax
import jax.numpy as jnp
from jax import lax
from jax.experimental import pallas as pl
from jax.experimental.pallas import tpu as pltpu

F32 = jnp.float32
BF16 = jnp.bfloat16

EPS = 1e-6
NEG_INF = -1e30
GMLP_CHUNK = 128
GMLP_GROUPS = 8
N_HEADS = 8
HEAD_DIM = 128
MOBA_BLOCK = 256
MOBA_TOPK = 3
ROPE_THETA = 10000.0
CONV_WIDTH = 3
SUBLANES = 8
BF16_SUBLANES = 16
VMEM_LIMIT = 56 * 1024 * 1024

GMLP_ROWS = 512
GMLP_OUT_GROUP = 4
FFN_ROWS = 512
FFN_CHUNK = 256
FFN_DOWN_GROUP = 5
QKV_ROWS = 1024
ATTN_HEADS_PER_STEP = 4


def _rms(x, g):
    return x * lax.rsqrt(jnp.mean(x * x, axis=-1, keepdims=True) + EPS) * g


def _gelu_times(x, half_factor):
    c = math.sqrt(2.0 / math.pi)
    inner = x * ((x * x) * (c * 0.044715) + c)
    return (x * (1.0 + jnp.tanh(inner))) * half_factor


def _gelu(x):
    return _gelu_times(x, 0.5)


def _dot(a, b):
    return jnp.dot(a, b, preferred_element_type=F32)


def _dot_nt(a, b):
    return lax.dot_general(a, b, (((1,), (1,)), ((), ())), preferred_element_type=F32)


def _const_spec(shape):
    return pl.BlockSpec(shape, lambda *_: (0,) * len(shape), pipeline_mode=pl.Buffered(1))


def _cast_plan(layer_weights, n_steps):
    in_specs, out_specs, out_shapes = [], [], []
    for stacked, layer in layer_weights:
        _, rows, cols = stacked.shape
        n_slices = max(s for s in range(1, n_steps + 1)
                       if rows % s == 0 and (rows // s) % BF16_SUBLANES == 0)
        last = n_slices - 1
        in_specs.append(pl.BlockSpec((None, rows // n_slices, cols),
                                     lambda i, layer=layer, last=last: (layer, jnp.minimum(i, last), 0)))
        out_specs.append(pl.BlockSpec((rows // n_slices, cols),
                                      lambda i, last=last: (jnp.minimum(i, last), 0)))
        out_shapes.append(jax.ShapeDtypeStruct((rows, cols), BF16))
    return in_specs, out_specs, out_shapes


def _cast_slices(src_refs, dst_refs):
    for src, dst in zip(src_refs, dst_refs, strict=True):
        dst[...] = src[...].astype(BF16)


def _gmlp_kernel(*refs, n_cast, out_group):
    h_ref, g_ref, win_ref, vg_ref, ws_ref, bst_ref, wout_ref = refs[:7]
    cast_src = refs[7:7 + n_cast]
    o_ref = refs[7 + n_cast]
    cast_dst = refs[8 + n_cast:8 + 2 * n_cast]
    v_scr = refs[8 + 2 * n_cast]
    _cast_slices(cast_src, cast_dst)
    tm = h_ref.shape[0]
    width = vg_ref.shape[1]
    gdim = width // GMLP_GROUPS
    h = h_ref[...]
    y = _rms(h, g_ref[...]).astype(BF16)
    v = _gelu(_dot(y, win_ref[:, width:]))
    v = _rms(v, vg_ref[...])
    v_scr[...] = v.astype(BF16)
    t_idx = lax.broadcasted_iota(jnp.int32, (GMLP_CHUNK, GMLP_CHUNK), 0)
    s_idx = lax.broadcasted_iota(jnp.int32, (GMLP_CHUNK, GMLP_CHUNK), 1)
    causal = s_idx <= t_idx

    def group_cols(g):
        return slice(g * gdim, (g + 1) * gdim)

    def group_dots(g):
        cols = group_cols(g)
        ws = jnp.where(causal, ws_ref[g] * 0.5, 0.0).astype(BF16)
        mixed = [_dot(ws, v_scr[c * GMLP_CHUNK:(c + 1) * GMLP_CHUNK, cols])
                 for c in range(tm // GMLP_CHUNK)]
        return _dot(y, win_ref[:, cols]), mixed

    acc = h
    nxt = group_dots(0)
    pending = []
    for g in range(GMLP_GROUPS):
        u_pre, mixed = nxt
        if g + 1 < GMLP_GROUPS:
            nxt = group_dots(g + 1)
        half_bias = bst_ref[:, g:g + 1] * 0.5
        half_s = jnp.concatenate([mx + half_bias for mx in mixed], axis=0)
        pending.append(_gelu_times(u_pre, half_s).astype(BF16))
        if len(pending) == out_group or g + 1 == GMLP_GROUPS:
            lo = (g + 1 - len(pending)) * gdim
            gated = pending[0] if len(pending) == 1 else jnp.concatenate(pending, axis=1)
            acc = acc + _dot(gated, wout_ref[lo:(g + 1) * gdim, :])
            pending = []
    o_ref[...] = acc


def _gmlp_layer(h, g, w_in, v_gain, w_s, b_s, w_out, cast=(), tm=GMLP_ROWS,
                out_group=GMLP_OUT_GROUP):
    n, d = h.shape
    width = v_gain.shape[-1]
    assert n % tm == 0 and tm % GMLP_CHUNK == 0 and w_s.shape == (GMLP_GROUPS, GMLP_CHUNK, GMLP_CHUNK)
    cast_in, cast_out, cast_shapes = _cast_plan(cast, n // tm)
    return pl.pallas_call(
        functools.partial(_gmlp_kernel, n_cast=len(cast), out_group=out_group),
        grid=(n // tm,),
        in_specs=[
            pl.BlockSpec((tm, d), lambda i: (i, 0)),
            _const_spec((1, d)),
            _const_spec((d, 2 * width)),
            _const_spec((1, width)),
            _const_spec(w_s.shape),
            _const_spec((GMLP_CHUNK, GMLP_GROUPS)),
            _const_spec((width, d)),
        ] + cast_in,
        out_specs=[pl.BlockSpec((tm, d), lambda i: (i, 0))] + cast_out,
        out_shape=[jax.ShapeDtypeStruct((n, d), F32)] + cast_shapes,
        scratch_shapes=[pltpu.VMEM((tm, width), BF16)],
        compiler_params=pltpu.CompilerParams(
            dimension_semantics=("arbitrary",), vmem_limit_bytes=VMEM_LIMIT),
        name="gmlp_mixer",
    )(h, g.reshape(1, d), w_in.astype(BF16), v_gain.reshape(1, width), w_s, b_s.T,
      w_out.astype(BF16), *[stacked for stacked, _ in cast])


def _ffn_kernel(*refs, tiles_per_seq, fc, down_group, final_norm, with_proj, n_cast):
    if with_proj:
        h_ref, attn_ref, wo_ref = refs[:3]
        refs = refs[3:]
    else:
        h_ref, h_next_ref = refs[:2]
        refs = refs[2:]
    g_ref, wup_ref, cw_ref, cb_ref, wdown_ref, fg_ref = refs[:6]
    cast_src = refs[6:6 + n_cast]
    o_ref = refs[6 + n_cast]
    cast_dst = refs[7 + n_cast:7 + 2 * n_cast]
    carry_ref = refs[7 + 2 * n_cast]
    y_buf = None if with_proj else refs[8 + 2 * n_cast]
    _cast_slices(cast_src, cast_dst)
    tm = h_ref.shape[0]
    d_ff = wdown_ref.shape[0]
    n_chunks = d_ff // fc
    seg = tm // SUBLANES

    def to_interleaved(x):
        return x.reshape(SUBLANES, seg, x.shape[1]).swapaxes(0, 1).reshape(tm, x.shape[1])

    def from_interleaved(x):
        return x.reshape(seg, SUBLANES, x.shape[1]).swapaxes(0, 1).reshape(tm, x.shape[1])

    @pl.when(pl.program_id(0) % tiles_per_seq == 0)
    def _():
        carry_ref[...] = jnp.zeros_like(carry_ref)

    def normed(x):
        return to_interleaved(_rms(x, g_ref[...])).astype(BF16)

    h = h_ref[...]
    if with_proj:
        h = h + _dot(attn_ref[...], wo_ref[...])
        y = normed(h)
        load_y = lambda: y
    else:
        step = pl.program_id(0)
        slot = step % 2

        @pl.when(step == 0)
        def _():
            y_buf[0] = normed(h)

        load_y = lambda: y_buf[slot]
    first_sublane = lax.broadcasted_iota(jnp.int32, (SUBLANES, fc), 0) == 0

    def cols_of(j, half):
        return slice(half * d_ff + j * fc, half * d_ff + (j + 1) * fc)

    def up_proj(j):
        return [_dot(load_y(), wup_ref[:, cols_of(j, half)]) for half in range(2)]

    def conv_act(j, a_pair):
        conv = []
        for half in range(2):
            cols = cols_of(j, half)
            a = a_pair[half]
            prev = carry_ref[:, cols]
            carry_ref[:, cols] = a[tm - 2 * SUBLANES:, :]
            wrap = [jnp.where(first_sublane,
                              pltpu.roll(prev[r * SUBLANES:(r + 1) * SUBLANES, :], 1, 0),
                              pltpu.roll(a[tm - (2 - r) * SUBLANES:tm - (1 - r) * SUBLANES, :], 1, 0))
                    for r in range(2)]
            taps = [jnp.concatenate([wrap[0], wrap[1], a[:tm - 2 * SUBLANES, :]], axis=0),
                    jnp.concatenate([wrap[1], a[:tm - SUBLANES, :]], axis=0),
                    a]
            scale = 1.0 if half == 0 else 0.5
            conv.append(sum((cw_ref[k:k + 1, cols] * scale) * taps[k] for k in range(CONV_WIDTH))
                        + cb_ref[:, cols] * scale)
        return _gelu_times(conv[0], conv[1]).astype(BF16)

    acc = None
    pending = []
    a_next = up_proj(0)
    for j in range(n_chunks):
        a_cur = a_next
        if y_buf is not None and j + 2 == n_chunks:
            y_buf[1 - slot] = normed(h_next_ref[...])
        if j + 1 < n_chunks:
            a_next = up_proj(j + 1)
        pending.append(conv_act(j, a_cur))
        if len(pending) == down_group or j + 1 == n_chunks:
            lo = (j + 1 - len(pending)) * fc
            act = pending[0] if len(pending) == 1 else jnp.concatenate(pending, axis=1)
            down = _dot(act, wdown_ref[lo:(j + 1) * fc, :])
            acc = down if acc is None else acc + down
            pending = []
    out = h + from_interleaved(acc)
    if final_norm:
        out = _rms(out, fg_ref[...])
    o_ref[...] = out


def _ffn_layer(h, g, w_up, conv_w, conv_b, w_down, final_g, seq, final_norm, attn=None, w_o=None,
               cast=(), tm=FFN_ROWS, fc=FFN_CHUNK, down_group=FFN_DOWN_GROUP):
    n, d = h.shape
    d_ff = w_down.shape[0]
    assert seq % tm == 0 and n % seq == 0 and d_ff % fc == 0 and conv_w.shape == (CONV_WIDTH, 2 * d_ff)
    with_proj = attn is not None
    kern = functools.partial(_ffn_kernel, tiles_per_seq=seq // tm, fc=fc, down_group=down_group,
                             final_norm=final_norm, with_proj=with_proj, n_cast=len(cast))
    row_spec = pl.BlockSpec((tm, d), lambda i: (i, 0))
    if with_proj:
        extra_specs, extra_args, extra_scratch = [row_spec, _const_spec((d, d))], [attn, w_o.astype(BF16)], []
    else:
        last = n // tm - 1
        extra_specs = [pl.BlockSpec((tm, d), lambda i: (jnp.minimum(i + 1, last), 0))]
        extra_args, extra_scratch = [h], [pltpu.VMEM((2, tm, d), BF16)]
    cast_in, cast_out, cast_shapes = _cast_plan(cast, n // tm)
    return pl.pallas_call(
        kern,
        grid=(n // tm,),
        in_specs=[row_spec] + extra_specs + [
            _const_spec((1, d)),
            _const_spec((d, 2 * d_ff)),
            _const_spec((CONV_WIDTH, 2 * d_ff)),
            _const_spec((1, 2 * d_ff)),
            _const_spec((d_ff, d)),
            _const_spec((1, d)),
        ] + cast_in,
        out_specs=[row_spec] + cast_out,
        out_shape=[jax.ShapeDtypeStruct((n, d), F32)] + cast_shapes,
        scratch_shapes=[pltpu.VMEM((2 * SUBLANES, 2 * d_ff), F32)] + extra_scratch,
        compiler_params=pltpu.CompilerParams(
            dimension_semantics=("arbitrary",), vmem_limit_bytes=VMEM_LIMIT),
        name="conv_ffn",
    )(h, *extra_args, g.reshape(1, d), w_up.astype(BF16), conv_w, conv_b.reshape(1, 2 * d_ff),
      w_down.astype(BF16), final_g.reshape(1, d), *[stacked for stacked, _ in cast])


def _qkv_kernel(h_ref, g_ref, w_ref, cos_ref, sin_ref, qh_ref, ql_ref, k_ref, vt_ref, km_ref):
    d = h_ref.shape[1]
    y = _rms(h_ref[...], g_ref[...]).astype(BF16)
    cos = cos_ref[...]
    sin = sin_ref[...]

    def rope(x):
        heads = []
        for hh in range(N_HEADS):
            xh = x[:, hh * HEAD_DIM:(hh + 1) * HEAD_DIM]
            heads.append(xh * cos + pltpu.roll(xh, HEAD_DIM // 2, 1) * sin)
        return jnp.concatenate(heads, axis=1)

    q = rope(_dot(y, w_ref[:, :d])) * (HEAD_DIM ** -0.5 * math.log2(math.e))
    qh = q.astype(BF16)
    qh_ref[...] = qh
    ql_ref[...] = (q - qh.astype(F32)).astype(BF16)
    k = rope(_dot(y, w_ref[:, d:2 * d]))
    k_ref[...] = k.astype(BF16)
    for r in range(km_ref.shape[0]):
        km_ref[r] = jnp.mean(k[r * MOBA_BLOCK:(r + 1) * MOBA_BLOCK, :], axis=0, keepdims=True)
    vt_ref[...] = _dot(y, w_ref[:, 2 * d:]).T.astype(BF16)


def _qkv_layer(h, g, w_qkv, cos2, sin2, seq, tm=QKV_ROWS):
    n, d = h.shape
    assert tm % MOBA_BLOCK == 0 and seq % tm == 0
    tps = seq // tm
    row_spec = pl.BlockSpec((tm, d), lambda i: (i, 0))
    tab_spec = pl.BlockSpec((tm, HEAD_DIM), lambda i: (i % tps, 0))
    return pl.pallas_call(
        _qkv_kernel,
        grid=(n // tm,),
        in_specs=[row_spec, _const_spec((1, d)), _const_spec((d, 3 * d)), tab_spec, tab_spec],
        out_specs=[row_spec, row_spec, row_spec,
                   pl.BlockSpec((d, tm), lambda i: (i // tps, i % tps)),
                   pl.BlockSpec((tm // MOBA_BLOCK, 1, d), lambda i: (i, 0, 0))],
        out_shape=[jax.ShapeDtypeStruct((n, d), BF16)] * 3
        + [jax.ShapeDtypeStruct((n // seq * d, seq), BF16),
           jax.ShapeDtypeStruct((n // MOBA_BLOCK, 1, d), F32)],
        compiler_params=pltpu.CompilerParams(
            dimension_semantics=("parallel",), vmem_limit_bytes=VMEM_LIMIT),
        name="qkv_rope",
    )(h, g.reshape(1, d), w_qkv.astype(BF16), cos2, sin2)


def _attn_kernel(qh_ref, ql_ref, k_ref, vt_ref, km_ref, o_ref):
    seq = qh_ref.shape[0]
    n_heads = qh_ref.shape[1] // HEAD_DIM
    nb = seq // MOBA_BLOCK
    blk = MOBA_BLOCK
    blk_id = lax.broadcasted_iota(jnp.int32, (nb, blk), 0)
    key_i = lax.broadcasted_iota(jnp.int32, (blk, blk), 0)
    qry_i = lax.broadcasted_iota(jnp.int32, (blk, blk), 1)
    causal = key_i <= qry_i

    def block(i):
        return slice(i * blk, (i + 1) * blk)

    def head(hh):
        return slice(hh * HEAD_DIM, (hh + 1) * HEAD_DIM)

    km2 = []
    for hh in range(n_heads):
        km = km_ref[:, 0, head(hh)]
        km_hi = km.astype(BF16).astype(F32)
        km2.append(jnp.concatenate([km_hi, km - km_hi], axis=0).astype(BF16))

    def scores(hh, i, n):
        return _dot_nt(k_ref[block(n), head(hh)], qh_ref[block(i), head(hh)])

    def gate_dots(hh, i):
        return (_dot_nt(km2[hh], qh_ref[block(i), head(hh)]),
                _dot_nt(km2[hh], ql_ref[block(i), head(hh)]))

    def softmax_shift(i, s, g):
        s_own = jnp.where(causal, s[i], NEG_INF)
        m = jnp.max(s_own, axis=0, keepdims=True)
        n_sel = min(MOBA_TOPK, i)
        if n_sel == 0:
            return s_own, [-m]
        g_hi, g_lo = g
        gate = g_hi[:nb, :] + (g_hi[nb:, :] + g_lo[:nb, :])
        gate = jnp.where(blk_id < i, gate, NEG_INF)
        rank = jnp.zeros((nb, blk), jnp.int32)
        for mm in range(i):
            gm = gate[mm:mm + 1, :]
            ahead = (gm > gate) | ((gm == gate) & (mm < blk_id))
            rank = rank + ahead.astype(jnp.int32)
        bias = jnp.where((rank < n_sel) & (blk_id < i), 0.0, NEG_INF)
        for n in range(i):
            m = jnp.maximum(m, jnp.max(s[n], axis=0, keepdims=True) + bias[n:n + 1, :])
        return s_own, [bias[n:n + 1, :] - m for n in range(i)] + [-m]

    items = [(hh, i) for i in range(nb) for hh in range(n_heads)]
    s_cur, g_cur = [scores(0, 0, 0)], None
    for t, (hh, i) in enumerate(items):
        s_own, shift = softmax_shift(i, s_cur, g_cur)
        s_cur[i] = s_own
        nxt = items[t + 1] if t + 1 < len(items) else None
        n_next = nxt[1] + 1 if nxt else 0
        s_next, g_next = [], None
        probs, l_sum = [], None
        for n in range(max(i + 1, n_next)):
            if n < n_next:
                s_next.append(scores(nxt[0], nxt[1], n))
                if n == 0 and nxt[1] > 0:
                    g_next = gate_dots(*nxt)
            if n <= i:
                p = jnp.exp2(s_cur[n] + shift[n])
                l_part = jnp.sum(p, axis=0, keepdims=True)
                l_sum = l_part if l_sum is None else l_sum + l_part
                probs.append(p.astype(BF16))
        acc = _dot(vt_ref[head(hh), 0:(i + 1) * blk], jnp.concatenate(probs, axis=0))
        o_ref[block(i), head(hh)] = (acc * (1.0 / l_sum)).T.astype(o_ref.dtype)
        s_cur, g_cur = s_next, g_next


def _attn_layer(qh, ql, k, vt, kmean, batch, seq, heads_per_step=ATTN_HEADS_PER_STEP):
    n, d = qh.shape
    nb = seq // MOBA_BLOCK
    width = heads_per_step * HEAD_DIM
    steps = N_HEADS // heads_per_step
    head_spec = pl.BlockSpec((seq, width), lambda b, h: (b, h))
    return pl.pallas_call(
        _attn_kernel,
        grid=(batch, steps),
        in_specs=[head_spec, head_spec, head_spec,
                  pl.BlockSpec((width, seq), lambda b, h: (b * steps + h, 0)),
                  pl.BlockSpec((nb, 1, width), lambda b, h: (b, 0, h))],
        out_specs=head_spec,
        out_shape=jax.ShapeDtypeStruct((n, d), BF16),
        compiler_params=pltpu.CompilerParams(
            dimension_semantics=("parallel", "parallel"), vmem_limit_bytes=VMEM_LIMIT),
        name="moba_attention",
    )(qh, ql, k, vt, kmean)


def _rope_tables(seq):
    half = HEAD_DIM // 2
    inv = ROPE_THETA ** (-jnp.arange(half, dtype=F32) / half)
    ang = jnp.arange(seq).astype(F32)[:, None] * inv[None, :]
    cos, sin = jnp.cos(ang), jnp.sin(ang)
    return jnp.concatenate([cos, cos], axis=-1), jnp.concatenate([-sin, sin], axis=-1)


def kernel(x, mix_norm, a_w_in, a_v_gain, a_w_s, a_b_s, a_w_out, b_w_qkv, b_w_o, ffn_norm,
           ffn_w_up, ffn_conv_w, ffn_conv_b, ffn_w_down, final_norm):
    batch, seq, d = x.shape
    assert seq % MOBA_BLOCK == 0 and d == N_HEADS * HEAD_DIM
    h = x.reshape(batch * seq, d)

    h, w_up0, w_down0 = _gmlp_layer(h, mix_norm[0], a_w_in[0], a_v_gain[0], a_w_s[0], a_b_s[0],
                                    a_w_out[0], cast=((ffn_w_up, 0), (ffn_w_down, 0)))
    h, w_qkv, w_o, w_up1, w_down1 = _ffn_layer(
        h, ffn_norm[0], w_up0, ffn_conv_w[0], ffn_conv_b[0], w_down0, final_norm, seq,
        final_norm=False, cast=((b_w_qkv, 0), (b_w_o, 0), (ffn_w_up, 1), (ffn_w_down, 1)))

    cos2, sin2 = _rope_tables(seq)
    qh, ql, k, vt, kmean = _qkv_layer(h, mix_norm[1], w_qkv, cos2, sin2, seq)
    attn = _attn_layer(qh, ql, k, vt, kmean, batch, seq)
    (h,) = _ffn_layer(h, ffn_norm[1], w_up1, ffn_conv_w[1], ffn_conv_b[1], w_down1,
                      final_norm, seq, final_norm=True, attn=attn, w_o=w_o)
    return h.reshape(batch, seq, d)
```
